```python
import functools
import jax, jax.numpy as jnp
from jax import lax
import numpy as np

D_MODEL = 1024
BATCH = 16
SEQ = 256
DEPTH = 1
DEC_BATCH = 2
DEC_SEQ = 4096
PAST_LEN = 256

GRID_W = 64
FOURIER_WIDTH = 512
FOURIER_GROUPS = 4
FOURIER_GROUP_DIM = FOURIER_WIDTH // FOURIER_GROUPS
RWKV_WIDTH = 1024
HEAD_DIM = 64
N_HEADS = RWKV_WIDTH // HEAD_DIM
DECAY_RANK = 64
AAA_RANK = 64
N_DIR = 2
N_BRANCH = 2
SHIFT_WIDTH = 3 * RWKV_WIDTH + DECAY_RANK + AAA_RANK
IN_WIDTH = 2 * FOURIER_WIDTH + SHIFT_WIDTH + RWKV_WIDTH + N_BRANCH * D_MODEL
RMS_EPS = 1e-6
GN_EPS = 64e-5

kernel_name = "hybrid_fnet_rwkv7_flow_step"


def _rmsnorm(x, g):
    xf = x.astype(jnp.float32)
    y = xf * lax.rsqrt(jnp.mean(xf * xf, axis=-1, keepdims=True) + RMS_EPS)
    return (y * g.astype(jnp.float32)).astype(x.dtype)


def _shift_context(u):
    B, T, C = u.shape
    v = u.reshape(B, T, C // 2, 2)
    prev = jnp.pad(v[:, :-1, :, 0], ((0, 0), (1, 0), (0, 0)))
    nxt = jnp.pad(v[:, 1:, :, 1], ((0, 0), (0, 1), (0, 0)))
    return jnp.stack([prev, nxt], axis=-1).reshape(B, T, C)


def _shift_grid(u, rows):
    B, T, C = u.shape
    v = u.reshape(B, rows, GRID_W, C // 4, 4)
    left = jnp.pad(v[:, :, :-1, :, 0], ((0, 0), (0, 0), (1, 0), (0, 0)))
    right = jnp.pad(v[:, :, 1:, :, 1], ((0, 0), (0, 0), (0, 1), (0, 0)))
    up = jnp.pad(v[:, :-1, :, :, 2], ((0, 0), (1, 0), (0, 0), (0, 0)))
    down = jnp.pad(v[:, 1:, :, :, 3], ((0, 0), (0, 1), (0, 0), (0, 0)))
    return jnp.stack([left, right, up, down], axis=-1).reshape(B, T, C)


def _fourier_mix(u):
    B, T, _ = u.shape
    ug = u.astype(jnp.float32).reshape(B, T, FOURIER_GROUPS, FOURIER_GROUP_DIM)
    f = jnp.fft.fftn(ug, axes=(1, 3), norm="ortho").real
    return f.reshape(B, T, FOURIER_WIDTH).astype(u.dtype)


def _rwkv7_scan(S0, r, w, k, v, a_vec, b_vec):
    def step(S, inp):
        r_t, w_t, k_t, v_t, a_t, b_t = inp
        sa = jnp.einsum('bhij,bhj->bhi', S, a_t)
        S = S * w_t[:, :, None, :] + sa[..., None] * b_t[:, :, None, :] + v_t[..., None] * k_t[:, :, None, :]
        y = jnp.einsum('bhij,bhj->bhi', S, r_t)
        return S, y
    xs = tuple(jnp.moveaxis(t, 1, 0) for t in (r, w, k, v, a_vec, b_vec))
    S_fin, ys = lax.scan(step, S0.astype(jnp.float32), xs)
    return S_fin, jnp.moveaxis(ys, 0, 1)


def _mixer(h, S0, shift_fn, w_in, mu_shift, w0, w_up, a0, a_up, k_k, k_a, r_k,
           lnx_g, lnx_b, w_proj_f, w_proj_r, w_out):
    B, T, _ = h.shape
    f32 = jnp.float32
    u = h @ w_in
    offs = [FOURIER_WIDTH, 2 * FOURIER_WIDTH, 2 * FOURIER_WIDTH + SHIFT_WIDTH,
            2 * FOURIER_WIDTH + SHIFT_WIDTH + RWKV_WIDTH]
    xf, gf, sh, gr, mg = jnp.split(u, offs, axis=-1)

    out_f = (_fourier_mix(xf) * jax.nn.silu(gf)) @ w_proj_f

    sh = sh + mu_shift * (shift_fn(sh) - sh)
    r, k, v, wd, ad = jnp.split(
        sh, [RWKV_WIDTH, 2 * RWKV_WIDTH, 3 * RWKV_WIDTH, 3 * RWKV_WIDTH + DECAY_RANK], axis=-1)
    heads = lambda t: t.astype(f32).reshape(t.shape[:-1] + (N_HEADS, HEAD_DIM))
    wl = (jnp.einsum('btr,zrc->zbtc', jnp.tanh(wd), w_up) + w0[:, None, None, :]).astype(f32)
    decay = heads(jnp.exp(-jnp.exp(-jax.nn.softplus(-wl) - 0.5)))
    a = heads(jax.nn.sigmoid(jnp.einsum('btr,zrc->zbtc', ad, a_up) + a0[:, None, None, :]))
    rh, kh, vh = heads(r), heads(k), heads(v)
    kk = heads(k * k_k)
    kk = kk / jnp.maximum(jnp.sqrt(jnp.sum(kk * kk, axis=-1, keepdims=True)), 1e-12)
    k_dir = kh[None] * (1.0 + (a - 1.0) * k_a.astype(f32).reshape(N_HEADS, HEAD_DIM))
    b_dir = kk[None] * a
    S_f, y_f = _rwkv7_scan(S0[:, 0], rh, decay[0], k_dir[0], vh, -kk, b_dir[0])
    flip = lambda t: jnp.flip(t, axis=1)
    S_b, y_b = _rwkv7_scan(S0[:, 1], flip(rh), flip(decay[1]), flip(k_dir[1]), flip(vh),
                           flip(-kk), flip(b_dir[1]))
    y = y_f + flip(y_b)
    mean = jnp.mean(y, axis=-1, keepdims=True)
    var = jnp.mean((y - mean) ** 2, axis=-1, keepdims=True)
    yn = ((y - mean) * lax.rsqrt(var + GN_EPS)).reshape(B, T, RWKV_WIDTH)
    yn = yn * lnx_g.astype(f32) + lnx_b.astype(f32)
    bonus = (jnp.sum(rh * kh * r_k.astype(f32), axis=-1, keepdims=True) * vh).reshape(B, T, RWKV_WIDTH)
    out_r = ((yn + bonus).astype(h.dtype) * jax.nn.silu(gr)) @ w_proj_r

    gates = jax.nn.sigmoid(mg.reshape(B, T, N_BRANCH, D_MODEL))
    merged = gates[..., 0, :] * out_f + gates[..., 1, :] * out_r
    return merged @ w_out, jnp.stack([S_f, S_b], axis=1)


def setup_inputs(seed: int = 0) -> dict:
    key = jax.random.key(seed)
    ks = jax.random.split(key, 24)
    nrm = lambda k, s, sc: jax.random.normal(k, s, jnp.float32) * sc
    D = D_MODEL
    return {
        "x_prompt": nrm(ks[0], (BATCH, SEQ, D), 1.0),
        "x_sample": nrm(ks[1], (DEC_BATCH, DEC_SEQ, D), 1.0),
        "state_rwkv": nrm(ks[2], (DEC_BATCH, DEPTH, N_DIR, N_HEADS, HEAD_DIM, HEAD_DIM), 0.3),
        "c": nrm(ks[3], (DEC_BATCH, D), 1.0),
        "c_ctx": nrm(ks[4], (D,), 1.0),
        "norm_g": 1.0 + nrm(ks[5], (DEPTH, D), 0.01),
        "w_ada": nrm(ks[6], (DEPTH, D, 3 * D), 0.5 * D ** -0.5),
        "b_ada": nrm(ks[7], (DEPTH, 3 * D), 0.01),
        "w_in": nrm(ks[8], (DEPTH, D, IN_WIDTH), D ** -0.5),
        "mu_shift": jax.random.uniform(ks[9], (DEPTH, SHIFT_WIDTH), jnp.float32),
        "w0": jax.random.uniform(ks[10], (DEPTH, N_DIR, RWKV_WIDTH), jnp.float32, -6.0, 1.0),
        "w_up": nrm(ks[11], (DEPTH, N_DIR, DECAY_RANK, RWKV_WIDTH), 0.1 * DECAY_RANK ** -0.5),
        "a0": nrm(ks[12], (DEPTH, N_DIR, RWKV_WIDTH), 0.5),
        "a_up": nrm(ks[13], (DEPTH, N_DIR, AAA_RANK, RWKV_WIDTH), 0.1 * AAA_RANK ** -0.5),
        "k_k": 0.85 + nrm(ks[14], (DEPTH, RWKV_WIDTH), 0.05),
        "k_a": 1.0 + nrm(ks[15], (DEPTH, RWKV_WIDTH), 0.05),
        "r_k": nrm(ks[16], (DEPTH, N_HEADS, HEAD_DIM), 0.1),
        "lnx_g": 1.0 + nrm(ks[17], (DEPTH, RWKV_WIDTH), 0.01),
        "lnx_b": nrm(ks[18], (DEPTH, RWKV_WIDTH), 0.01),
        "w_proj_f": nrm(ks[19], (DEPTH, FOURIER_WIDTH, D), FOURIER_WIDTH ** -0.5),
        "w_proj_r": nrm(ks[20], (DEPTH, RWKV_WIDTH, D), RWKV_WIDTH ** -0.5),
        "w_out": nrm(ks[21], (DEPTH, D, D), D ** -0.5),
        "final_g": 1.0 + nrm(ks[22], (D,), 0.01),
    }


def reference(x_prompt, x_sample, state_rwkv, c, c_ctx, norm_g, w_ada, b_ada, w_in, mu_shift,
              w0, w_up, a0, a_up, k_k, k_a, r_k, lnx_g, lnx_b, w_proj_f, w_proj_r, w_out, final_g):
    rows = x_sample.shape[1] // GRID_W
    grid_shift = functools.partial(_shift_grid, rows=rows)
    xp, xs = x_prompt, x_sample
    S_ctx0 = jnp.zeros((xp.shape[0], N_DIR, N_HEADS, HEAD_DIM, HEAD_DIM), jnp.float32)
    new_states = []
    for l in range(DEPTH):
        params = (w_in[l], mu_shift[l], w0[l], w_up[l], a0[l], a_up[l], k_k[l], k_a[l], r_k[l],
                  lnx_g[l], lnx_b[l], w_proj_f[l], w_proj_r[l], w_out[l])
        m_ctx = jax.nn.silu(c_ctx) @ w_ada[l] + b_ada[l]
        sft, scl, gte = jnp.split(m_ctx, 3, axis=-1)
        hp = _rmsnorm(xp, norm_g[l]) * (1.0 + scl) + sft
        op, S_ctx = _mixer(hp, S_ctx0, _shift_context, *params)
        xp = xp + gte * op
        new_states.append(S_ctx)
        m_lat = jax.nn.silu(c) @ w_ada[l] + b_ada[l]
        sft_s, scl_s, gte_s = jnp.split(m_lat[:, None, :], 3, axis=-1)
        hs = _rmsnorm(xs, norm_g[l]) * (1.0 + scl_s) + sft_s
        os_, _ = _mixer(hs, state_rwkv[:, l], grid_shift, *params)
        xs = xs + gte_s * os_
    y_prompt = _rmsnorm(xp, final_g)
    y_sample = _rmsnorm(xs, final_g)
    new_state_rwkv = jnp.stack(new_states, axis=1)
    return (y_prompt, y_sample, new_state_rwkv)
```

```python
import functools
import math

import numpy as np
import jax
import jax.numpy as jnp
from jax import lax
from jax.experimental import pallas as pl
from jax.experimental.pallas import tpu as pltpu

F32 = jnp.float32
BF16 = jnp.bfloat16

LANES = 128
VMEM_LIMIT = 56 * 1024 * 1024

HEAD = 64
GRID_W = 64
FOURIER_GROUP = 128
RMS_EPS = 1e-6
GN_EPS = 64e-5

TBLK = 128
CHUNK = 32


def _cparams(sem):
    return pltpu.CompilerParams(dimension_semantics=sem, vmem_limit_bytes=VMEM_LIMIT)


def _bdot(a, b):
    return jnp.dot(a.astype(BF16), b.astype(BF16), preferred_element_type=F32)


def _bdot_nt(a, b):
    return lax.dot_general(a.astype(BF16), b.astype(BF16), (((1,), (1,)), ((), ())),
                           preferred_element_type=F32)


def _bdot_tn(a, b):
    return lax.dot_general(a.astype(BF16), b.astype(BF16), (((0,), (0,)), ((), ())),
                           preferred_element_type=F32)


def _split2(x):
    hi = x.astype(BF16)
    lo = (x - hi.astype(F32)).astype(BF16)
    return hi, lo


def _dot_exact_rhs(x, m):
    hi, lo = _split2(x)
    return (jnp.dot(hi, m, preferred_element_type=F32) + jnp.dot(lo, m, preferred_element_type=F32))


def _dot_exact_lhs(m, x):
    hi, lo = _split2(x)
    return (jnp.dot(m, hi, preferred_element_type=F32) + jnp.dot(m, lo, preferred_element_type=F32))


def _silu(x):
    return x * jax.nn.sigmoid(x)


def _mod_kernel(c_ref, w_ref, b_ref, o_ref):
    c = c_ref[...]
    o_ref[...] = jnp.dot(_silu(c), w_ref[...], precision=lax.Precision.HIGHEST,
                         preferred_element_type=F32) + b_ref[...]


def _modulation(cvec, w_ada, b_ada):
    rows, d = cvec.shape
    n = w_ada.shape[1]
    tn = 512
    return pl.pallas_call(
        _mod_kernel,
        out_shape=jax.ShapeDtypeStruct((rows, n), F32),
        grid=(n // tn,),
        in_specs=[pl.BlockSpec((rows, d), lambda j: (0, 0)),
                  pl.BlockSpec((d, tn), lambda j: (0, j)),
                  pl.BlockSpec((1, tn), lambda j: (0, j))],
        out_specs=pl.BlockSpec((rows, tn), lambda j: (0, j)),
        compiler_params=_cparams(("arbitrary",)),
        name="adaln_modulation",
    )(cvec, w_ada, b_ada.reshape(1, n))


def _inproj_kernel(x_ref, mod_ref, g_ref, w_ref, o_ref, h_ref):
    @pl.when(pl.program_id(2) == 0)
    def _():
        x = x_ref[0]
        y = x * lax.rsqrt(jnp.mean(x * x, axis=-1, keepdims=True) + RMS_EPS) * g_ref[...]
        m = mod_ref[0]
        h_ref[...] = (y * (1.0 + m[1:2]) + m[0:1]).astype(BF16)

    o_ref[0] = jnp.dot(h_ref[...], w_ref[...], preferred_element_type=F32)


def _inproj(x, mod, norm_g, w_bf16, tm, tn):
    b, t, d = x.shape
    n = w_bf16.shape[1]
    return pl.pallas_call(
        _inproj_kernel,
        out_shape=jax.ShapeDtypeStruct((b, t, n), F32),
        grid=(b, t // tm, n // tn),
        in_specs=[pl.BlockSpec((1, tm, d), lambda bi, i, j: (bi, i, 0)),
                  pl.BlockSpec((1, 8, d), lambda bi, i, j: (bi, 0, 0)),
                  pl.BlockSpec((1, d), lambda bi, i, j: (0, 0)),
                  pl.BlockSpec((d, tn), lambda bi, i, j: (0, j))],
        out_specs=pl.BlockSpec((1, tm, tn), lambda bi, i, j: (bi, i, j)),
        scratch_shapes=[pltpu.VMEM((tm, d), BF16)],
        compiler_params=_cparams(("arbitrary", "arbitrary", "arbitrary")),
        name="norm_mod_inproj",
    )(x, mod, norm_g.reshape(1, d), w_bf16)


def _fourier_kernel(xf_ref, gf_ref, cs_ref, cc_ref, o_ref, pq_ref, *, t, rows_per_step):
    @pl.when(pl.program_id(1) == 0)
    def _():
        def body(i, carry):
            r0 = pl.multiple_of(i * rows_per_step, rows_per_step)
            for g in range(xf_ref.shape[2] // FOURIER_GROUP):
                sl = slice(g * FOURIER_GROUP, (g + 1) * FOURIER_GROUP)
                xg = xf_ref[0, pl.ds(r0, rows_per_step), sl]
                pq = _bdot(xg, cc_ref[...])
                pq_ref[pl.ds(r0, rows_per_step), sl] = pq[:, :FOURIER_GROUP].astype(BF16)
                pq_ref[pl.ds(t + r0, rows_per_step), sl] = pq[:, FOURIER_GROUP:].astype(BF16)
            return carry
        lax.fori_loop(0, t // rows_per_step, body, 0)

    f = jnp.dot(cs_ref[...], pq_ref[...], preferred_element_type=F32)
    o_ref[0] = f * _silu(gf_ref[0])


def _dft_mats(t):
    def cs(n):
        i = jnp.arange(n, dtype=jnp.int32)
        ang = ((i[:, None] * i[None, :]) % n).astype(F32) * (2.0 * math.pi / n)
        s = 1.0 / math.sqrt(n)
        return jnp.cos(ang) * s, jnp.sin(ang) * s
    ct, st = cs(t)
    cg, sg = cs(FOURIER_GROUP)
    return (jnp.concatenate([ct, -st], axis=1).astype(BF16),
            jnp.concatenate([cg, sg], axis=1).astype(BF16))


def _fourier(u, t, width, tf):
    b = u.shape[0]
    cs, cc = _dft_mats(t)
    kern = functools.partial(_fourier_kernel, t=t, rows_per_step=min(t, 256))
    return pl.pallas_call(
        kern,
        out_shape=jax.ShapeDtypeStruct((b, t, width), F32),
        grid=(b, t // tf),
        in_specs=[pl.BlockSpec((1, t, width), lambda bi, i: (bi, 0, 0)),
                  pl.BlockSpec((1, tf, width), lambda bi, i: (bi, i, 1)),
                  pl.BlockSpec((tf, 2 * t), lambda bi, i: (i, 0)),
                  pl.BlockSpec((FOURIER_GROUP, 2 * FOURIER_GROUP), lambda bi, i: (0, 0))],
        out_specs=pl.BlockSpec((1, tf, width), lambda bi, i: (bi, i, 0)),
        scratch_shapes=[pltpu.VMEM((2 * t, width), BF16)],
        compiler_params=_cparams(("arbitrary", "arbitrary")),
        name="fourier_mix",
    )(u, u, cs, cc)


def _shift(x, prev_halo, next_halo, t_total, row0, grid_shift):
    rows, lanes = x.shape
    row = lax.broadcasted_iota(jnp.int32, (rows, lanes), 0)
    lane = lax.broadcasted_iota(jnp.int32, (rows, lanes), 1)
    before = pltpu.roll(x, 1, 0)
    after = pltpu.roll(x, rows - 1, 0)
    if not grid_shift:
        prev = jnp.where(row == 0, 0.0, before)
        nxt = jnp.where(row == rows - 1, 0.0, after)
        return jnp.where((lane & 1) == 0, prev, nxt)
    col = row & (GRID_W - 1)
    grow = row + row0
    left = jnp.where(col == 0, 0.0, before)
    right = jnp.where(col == GRID_W - 1, 0.0, after)
    up = jnp.concatenate([prev_halo, x[:rows - GRID_W]], axis=0)
    up = jnp.where(grow < GRID_W, 0.0, up)
    down = jnp.concatenate([x[GRID_W:], next_halo], axis=0)
    down = jnp.where(grow >= t_total - GRID_W, 0.0, down)
    m = lane & 3
    return jnp.where(m == 0, left, jnp.where(m == 1, right, jnp.where(m == 2, up, down)))


def _prep_kernel(*refs, grid_shift, t_total, width, cw):
    if grid_shift:
        (r_ref, k_ref, v_ref, d_ref, rp_ref, kp_ref, vp_ref, dp_ref, rn_ref, kn_ref, vn_ref, dn_ref,
         mu_ref, mud_ref, kk_ref, wcat_ref, bias_ref, seg_ref,
         ro_ref, ko_ref, vo_ref, kko_ref, lwf_ref, lwb_ref, af_ref, ab_ref) = refs
    else:
        (r_ref, k_ref, v_ref, d_ref, mu_ref, mud_ref, kk_ref, wcat_ref, bias_ref, seg_ref,
         ro_ref, ko_ref, vo_ref, kko_ref, lwf_ref, lwb_ref, af_ref, ab_ref) = refs
        rp_ref = kp_ref = vp_ref = dp_ref = rn_ref = kn_ref = vn_ref = dn_ref = None
    rows = r_ref.shape[1]
    row0 = pl.program_id(1) * rows

    def lerp(x_ref, p_ref, n_ref, mu, sl):
        x = x_ref[0, :, sl]
        ph = p_ref[0, :, sl] if grid_shift else None
        nh = n_ref[0, :, sl] if grid_shift else None
        s = _shift(x, ph, nh, t_total, row0, grid_shift)
        return x + mu * (s - x)

    for j in range(width // cw):
        sl = slice(j * cw, (j + 1) * cw)
        ro_ref[0, :, sl] = lerp(r_ref, rp_ref, rn_ref, mu_ref[0:1, sl], sl)
        vo_ref[0, :, sl] = lerp(v_ref, vp_ref, vn_ref, mu_ref[2:3, sl], sl)
        k = lerp(k_ref, kp_ref, kn_ref, mu_ref[1:2, sl], sl)
        ko_ref[0, :, sl] = k
        kk = k * kk_ref[0:1, sl]
        ss = _dot_exact_rhs(kk * kk, seg_ref[...])
        kko_ref[0, :, sl] = kk / jnp.maximum(jnp.sqrt(ss), 1e-12)

    dsl = slice(0, LANES)
    d = lerp(d_ref, dp_ref, dn_ref, mud_ref[...], dsl)
    lane = lax.broadcasted_iota(jnp.int32, d.shape, 1)
    d = jnp.where(lane < HEAD, jnp.tanh(d), d).astype(BF16)
    outs = (lwf_ref, lwb_ref, af_ref, ab_ref)
    for z in range(4):
        for j in range(width // cw):
            sl = slice(j * cw, (j + 1) * cw)
            wsl = slice(z * width + j * cw, z * width + (j + 1) * cw)
            pre = jnp.dot(d, wcat_ref[:, wsl], preferred_element_type=F32) + bias_ref[z:z + 1, sl]
            sg = jax.nn.sigmoid(pre)
            outs[z][0, :, sl] = sg * (-math.exp(-0.5)) if z < 2 else sg


def _rwkv_prep(u, mu_shift, k_k, w_up, a_up, w0, a0, grid_shift, tt):
    b, t, _ = u.shape
    width = k_k.shape[0]
    cw = 256
    mu = mu_shift[:3 * width].reshape(3, width)
    mud = mu_shift[3 * width:].reshape(1, LANES)
    zpad = jnp.zeros((HEAD, width), F32)
    wcat = jnp.concatenate([jnp.concatenate([w_up[0], zpad], 0), jnp.concatenate([w_up[1], zpad], 0),
                            jnp.concatenate([zpad, a_up[0]], 0), jnp.concatenate([zpad, a_up[1]], 0)],
                           axis=1).astype(BF16)
    bias = jnp.concatenate([w0, a0], axis=0)
    seg = _seg_matrix(cw)
    d_blk = (u.shape[2] - LANES) // LANES
    main = [pl.BlockSpec((1, tt, width), lambda bi, i, c=c: (bi, i, c)) for c in (1, 2, 3)]
    main.append(pl.BlockSpec((1, tt, LANES), lambda bi, i: (bi, i, d_blk)))
    ins = [u, u, u, u]
    specs = list(main)
    if grid_shift:
        hb = tt // GRID_W
        last = t // GRID_W - 1
        for off in (-1, hb):
            def imap(bi, i, c, off=off):
                return (bi, jnp.clip(i * hb + off, 0, last), c)
            specs += [pl.BlockSpec((1, GRID_W, width), functools.partial(imap, c=c)) for c in (1, 2, 3)]
            specs.append(pl.BlockSpec((1, GRID_W, LANES), functools.partial(imap, c=d_blk)))
            ins += [u, u, u, u]
    const = lambda shape: pl.BlockSpec(shape, lambda bi, i: (0, 0))
    specs += [const((3, width)), const((1, LANES)), const((1, width)), const((LANES, 4 * width)),
              const((4, width)), const((cw, cw))]
    ins += [mu, mud, k_k.reshape(1, width), wcat, bias, seg]
    out_spec = pl.BlockSpec((1, tt, width), lambda bi, i: (bi, i, 0))
    kern = functools.partial(_prep_kernel, grid_shift=grid_shift, t_total=t, width=width, cw=cw)
    return pl.pallas_call(
        kern,
        out_shape=[jax.ShapeDtypeStruct((b, t, width), F32)] * 8,
        grid=(b, t // tt),
        in_specs=specs,
        out_specs=[out_spec] * 8,
        compiler_params=_cparams(("arbitrary", "arbitrary")),
        name="rwkv_prep",
    )(*ins)


def _seg_matrix(n):
    i = np.arange(n) // HEAD
    return jnp.asarray((i[:, None] == i[None, :]).astype(np.float32), dtype=BF16)


def _scan_consts():
    t = np.arange(TBLK)
    same = (t[:, None] // CHUNK) == (t[None, :] // CHUNK)
    tri_f = same & (t[None, :] <= t[:, None])
    tri_b = same & (t[None, :] >= t[:, None])
    mk = lambda m: jnp.asarray(m.astype(np.float32), dtype=BF16)
    return mk(tri_f), mk(tri_b), mk(same)


def _scan_block(r, k, v, kk, lw, a, k_a, state, tri, blk, reverse):
    n = TBLK
    kd = k * (1.0 + (a - 1.0) * k_a)
    b = kk * a
    cl = _dot_exact_lhs(tri, lw)
    tot = _dot_exact_lhs(blk, lw)
    e_in = jnp.exp(cl)
    at = -kk * jnp.exp(cl - lw)
    rt = r * e_in
    e_neg = jnp.exp(-cl)
    bt = b * e_neg
    kt = kd * e_neg
    e_tail = jnp.exp(tot - cl)
    bh = b * e_tail
    kh = kd * e_tail
    w_chunk = jnp.exp(tot)

    row = lax.broadcasted_iota(jnp.int32, (n, n), 0)
    col = lax.broadcasted_iota(jnp.int32, (n, n), 1)
    same = (row // CHUNK) == (col // CHUNK)
    if reverse:
        strict = same & (col > row)
        incl = same & (col >= row)
    else:
        strict = same & (col < row)
        incl = same & (col <= row)
    lane = lax.broadcasted_iota(jnp.int32, (n, LANES), 1)
    head_masks = (lane < HEAD, lane >= HEAD)

    lhs = jnp.concatenate([at, rt], axis=0)
    rhs = jnp.concatenate([bt, kt], axis=0).astype(BF16)
    v16 = v.astype(BF16)
    a_hat = jnp.zeros((n, LANES), F32)
    r_hat = rt
    uv = jnp.zeros((n, LANES), F32)
    yv = jnp.zeros((n, LANES), F32)
    for hm in head_masks:
        hm2 = jnp.concatenate([hm, hm], axis=0)
        g = _bdot_nt(jnp.where(hm2, lhs, 0.0), rhs)
        a_ab = jnp.where(strict, g[:n, :n], 0.0)
        a_ak = jnp.where(strict, g[:n, n:], 0.0)
        a_rb = jnp.where(incl, g[n:, :n], 0.0)
        a_rk = jnp.where(incl, g[n:, n:], 0.0)
        q = a_ab
        x = a_ab
        for _ in range(int(math.log2(CHUNK)) - 1):
            x = _bdot(x, x)
            q = q + x + _bdot(q, x)
        at_h = jnp.where(hm, at, 0.0)
        ah_h = at_h + _bdot(q, at_h)
        akv = _bdot(a_ak, v16)
        uv_h = akv + _bdot(q, akv)
        both = _bdot(a_rb, jnp.concatenate([ah_h, uv_h], axis=1))
        a_hat = a_hat + ah_h
        r_hat = r_hat + both[:, :LANES]
        uv = uv + jnp.where(hm, uv_h, 0.0)
        yv = yv + jnp.where(hm, both[:, LANES:] + _bdot(a_rk, v16), 0.0)

    srow = lax.broadcasted_iota(jnp.int32, (LANES, LANES), 0)
    scol = lax.broadcasted_iota(jnp.int32, (LANES, LANES), 1)
    pair_diag = (srow < HEAD) == (scol < HEAD)
    n_chunks = n // CHUNK
    order = range(n_chunks - 1, -1, -1) if reverse else range(n_chunks)
    ys = [None] * n_chunks
    for c in order:
        sl = slice(c * CHUNK, (c + 1) * CHUNK)
        proj = _bdot_nt(jnp.concatenate([a_hat[sl], r_hat[sl]], axis=0), state)
        u_c = proj[:CHUNK] + uv[sl]
        ys[c] = proj[CHUNK:] + yv[sl]
        upd = _bdot_tn(jnp.concatenate([u_c, v[sl]], axis=0), jnp.concatenate([bh[sl], kh[sl]], axis=0))
        state = state * w_chunk[c * CHUNK:c * CHUNK + 1] + jnp.where(pair_diag, upd, 0.0)
    return jnp.concatenate(ys, axis=0), state


def _scan_kernel(*refs, has_init, want_state):
    (rf, kf, vf, kkf, lwf, af, rb, kb, vb, kkb, lwb, ab, ka_ref, trif_ref, trib_ref, blk_ref) = refs[:16]
    pos = 16
    s0_ref = None
    if has_init:
        s0_ref = refs[pos]
        pos += 1
    yf_ref, yb_ref = refs[pos:pos + 2]
    pos += 2
    so_ref = None
    if want_state:
        so_ref = refs[pos]
        pos += 1
    st_ref = refs[pos]
    i = pl.program_id(2)

    @pl.when(i == 0)
    def _():
        if has_init:
            st_ref[...] = s0_ref[0, :, 0]
        else:
            st_ref[...] = jnp.zeros(st_ref.shape, F32)

    k_a = ka_ref[...]
    blk = blk_ref[...]
    y, s = _scan_block(rf[0], kf[0], vf[0], kkf[0], lwf[0], af[0], k_a, st_ref[0], trif_ref[...], blk, False)
    yf_ref[0] = y
    st_ref[0] = s
    y, s = _scan_block(rb[0], kb[0], vb[0], kkb[0], lwb[0], ab[0], k_a, st_ref[1], trib_ref[...], blk, True)
    yb_ref[0] = y
    st_ref[1] = s

    if want_state:
        @pl.when(i == pl.num_programs(2) - 1)
        def _():
            so_ref[0, :, 0] = st_ref[...]


def _rwkv_scan(r, k, v, kk, lw_f, lw_b, a_f, a_b, k_a, s0, want_state):
    b, t, width = r.shape
    pairs = width // LANES
    nt = t // TBLK
    fwd = pl.BlockSpec((1, TBLK, LANES), lambda bi, p, i: (bi, i, p))
    bwd = pl.BlockSpec((1, TBLK, LANES), lambda bi, p, i: (bi, nt - 1 - i, p))
    cmat = pl.BlockSpec((TBLK, TBLK), lambda bi, p, i: (0, 0))
    st_spec = pl.BlockSpec((1, 2, 1, LANES, LANES), lambda bi, p, i: (bi, 0, p, 0, 0))
    specs = [fwd] * 6 + [bwd] * 6 + [pl.BlockSpec((1, LANES), lambda bi, p, i: (0, p)), cmat, cmat, cmat]
    ins = [r, k, v, kk, lw_f, a_f, r, k, v, kk, lw_b, a_b, k_a.reshape(1, width), *_scan_consts()]
    if s0 is not None:
        specs.append(st_spec)
        ins.append(s0)
    y_shape = jax.ShapeDtypeStruct((b, t, width), F32)
    out_shape = [y_shape, y_shape]
    out_specs = [fwd, bwd]
    if want_state:
        out_shape.append(jax.ShapeDtypeStruct((b, 2, pairs, LANES, LANES), F32))
        out_specs.append(st_spec)
    kern = functools.partial(_scan_kernel, has_init=s0 is not None, want_state=want_state)
    return pl.pallas_call(
        kern,
        out_shape=out_shape,
        grid=(b, pairs, nt),
        in_specs=specs,
        out_specs=out_specs,
        scratch_shapes=[pltpu.VMEM((2, LANES, LANES), F32)],
        compiler_params=_cparams(("arbitrary", "arbitrary", "arbitrary")),
        name="rwkv7_scan",
    )(*ins)


def _pair_states(s):
    b, z, h, n, _ = s.shape
    sp = s.reshape(b, z, h // 2, 2, n, n)
    zero = jnp.zeros_like(sp[:, :, :, 0])
    top = jnp.concatenate([sp[:, :, :, 0], zero], axis=-1)
    bot = jnp.concatenate([zero, sp[:, :, :, 1]], axis=-1)
    return jnp.concatenate([top, bot], axis=-2)


def _unpair_states(sp):
    n = HEAD
    b, z, p = sp.shape[:3]
    return jnp.stack([sp[..., :n, :n], sp[..., n:, n:]], axis=3).reshape(b, z, 2 * p, n, n)


def _post_kernel(yf_ref, yb_ref, r_ref, k_ref, v_ref, gr_ref, g0_ref, g1_ref, fo_ref, x_ref, mod_ref,
                 vec_ref, seg_ref, wf_ref, wr_ref, wo_ref, o_ref, z_ref, *, cw):
    width = yf_ref.shape[2]
    inv_n = 1.0 / HEAD
    for j in range(width // cw):
        sl = slice(j * cw, (j + 1) * cw)
        seg = seg_ref[...]
        y = yf_ref[0, :, sl] + yb_ref[0, :, sl]
        mean = _dot_exact_rhs(y, seg) * inv_n
        d = y - mean
        var = _dot_exact_rhs(d * d, seg) * inv_n
        yn = d * lax.rsqrt(var + GN_EPS) * vec_ref[0:1, sl] + vec_ref[1:2, sl]
        rk = _dot_exact_rhs(r_ref[0, :, sl] * k_ref[0, :, sl] * vec_ref[2:3, sl], seg)
        z = (yn + rk * v_ref[0, :, sl]) * _silu(gr_ref[0, :, sl])
        z_ref[:, sl] = z.astype(BF16)
    out_r = jnp.dot(z_ref[...], wr_ref[...], preferred_element_type=F32)
    out_f = jnp.dot(fo_ref[0].astype(BF16), wf_ref[...], preferred_element_type=F32)
    merged = jax.nn.sigmoid(g0_ref[0]) * out_f + jax.nn.sigmoid(g1_ref[0]) * out_r
    o = jnp.dot(merged.astype(BF16), wo_ref[...], preferred_element_type=F32)
    xn = x_ref[0] + mod_ref[0][2:3] * o
    o_ref[0] = xn * lax.rsqrt(jnp.mean(xn * xn, axis=-1, keepdims=True) + RMS_EPS) * vec_ref[3:4, :]


def _post(yf, yb, r, k, v, u, fo, x, mod, lnx_g, lnx_b, r_k, final_g, w_proj_f, w_proj_r, w_out, tm):
    b, t, d = x.shape
    width = yf.shape[2]
    fw = fo.shape[2]
    cw = 256
    vec = jnp.stack([lnx_g, lnx_b, r_k.reshape(-1), final_g], axis=0)
    tok = lambda w, c: pl.BlockSpec((1, tm, w), lambda bi, i: (bi, i, c))
    const = lambda shape: pl.BlockSpec(shape, lambda bi, i: (0,) * len(shape))
    gr_off = 2 * fw + 3 * width
    mg_off = gr_off + width
    assert gr_off % width == 0 and mg_off % d == 0
    gr_blk, mg_blk = gr_off // width, mg_off // d
    specs = [tok(width, 0)] * 5 + [tok(width, gr_blk), tok(d, mg_blk), tok(d, mg_blk + 1), tok(fw, 0),
                                   tok(d, 0), pl.BlockSpec((1, 8, d), lambda bi, i: (bi, 0, 0)),
                                   const((4, d)), const((cw, cw)), const((fw, d)), const((width, d)),
                                   const((d, d))]
    return pl.pallas_call(
        functools.partial(_post_kernel, cw=cw),
        out_shape=jax.ShapeDtypeStruct((b, t, d), F32),
        grid=(b, t // tm),
        in_specs=specs,
        out_specs=tok(d, 0),
        scratch_shapes=[pltpu.VMEM((tm, width), BF16)],
        compiler_params=_cparams(("arbitrary", "arbitrary")),
        name="post_mix",
    )(yf, yb, r, k, v, u, u, u, fo, x, mod, vec, _seg_matrix(cw),
      w_proj_f.astype(BF16), w_proj_r.astype(BF16), w_out.astype(BF16))


def _mixer_path(x, mod, s0, want_state, grid_shift, p, tm, tn, tf, tt, tpost):
    b, t, d = x.shape
    u = _inproj(x, mod, p["norm_g"], p["w_in"], tm, tn)
    fo = _fourier(u, t, p["fourier_width"], tf)
    r, k, v, kk, lw_f, lw_b, a_f, a_b = _rwkv_prep(
        u, p["mu_shift"], p["k_k"], p["w_up"], p["a_up"], p["w0"], p["a0"], grid_shift, tt)
    res = _rwkv_scan(r, k, v, kk, lw_f, lw_b, a_f, a_b, p["k_a"], s0, want_state)
    y = _post(res[0], res[1], r, k, v, u, fo, x, mod, p["lnx_g"], p["lnx_b"], p["r_k"], p["final_g"],
              p["w_proj_f"], p["w_proj_r"], p["w_out"], tpost)
    return y, (res[2] if want_state else None)


def kernel(x_prompt, x_sample, state_rwkv, c, c_ctx, norm_g, w_ada, b_ada, w_in, mu_shift, w0, w_up, a0,
           a_up, k_k, k_a, r_k, lnx_g, lnx_b, w_proj_f, w_proj_r, w_out, final_g):
    depth = w_in.shape[0]
    assert depth == 1, "the final norm is fused into the single layer's post kernel"
    bp, tp, d = x_prompt.shape
    bs, ts, _ = x_sample.shape
    width = k_k.shape[1]
    fw = w_proj_f.shape[1]
    l = 0
    sh_end = 2 * fw + 3 * width
    rank2 = mu_shift.shape[1] - 3 * width
    w_in_l = jnp.concatenate([w_in[l][:, :sh_end], w_in[l][:, sh_end + rank2:],
                              w_in[l][:, sh_end:sh_end + rank2]], axis=1).astype(BF16)
    p = dict(norm_g=norm_g[l], w_in=w_in_l, fourier_width=fw, mu_shift=mu_shift[l], k_k=k_k[l], w_up=w_up[l],
             a_up=a_up[l], w0=w0[l], a0=a0[l], k_a=k_a[l], lnx_g=lnx_g[l], lnx_b=lnx_b[l], r_k=r_k[l],
             final_g=final_g, w_proj_f=w_proj_f[l], w_proj_r=w_proj_r[l], w_out=w_out[l])

    rows = 8
    cvec = jnp.concatenate([c_ctx[None], c, jnp.zeros((rows - 1 - bs, d), F32)], axis=0)
    m = _modulation(cvec, w_ada[l], b_ada[l])
    m3 = m.reshape(rows, 3, d)
    mod = jnp.concatenate([m3, jnp.zeros((rows, 5, d), F32)], axis=1)

    tn = 384
    yp, sp = _mixer_path(x_prompt, jnp.broadcast_to(mod[0:1], (bp, 8, d)), None, True, False, p,
                         tm=tp, tn=tn, tf=tp, tt=tp, tpost=tp)
    s0 = _pair_states(state_rwkv[:, l])
    ys, _ = _mixer_path(x_sample, mod[1:1 + bs], s0, False, True, p,
                        tm=512, tn=tn, tf=256, tt=256, tpost=256)
    new_state = _unpair_states(sp)[:, None]
    return yp, ys, new_state
```

```python
import functools
import math

import numpy as np
import jax
import jax.numpy as jnp
from jax import lax
from jax.experimental import pallas as pl
from jax.experimental.pallas import tpu as pltpu

F32 = jnp.float32
BF16 = jnp.bfloat16

LANES = 128
MXU_DIM = 256
VMEM_LIMIT = 56 * 1024 * 1024

HEAD = 64
GRID_W = 64
FOURIER_GROUP = 128
RMS_EPS = 1e-6
GN_EPS = 64e-5

TBLK = 128
CHUNK = 32
SCAN_PAIRS = 4


def _cparams(sem):
    return pltpu.CompilerParams(dimension_semantics=sem, vmem_limit_bytes=VMEM_LIMIT)


def _bdot(a, b):
    return jnp.dot(a.astype(BF16), b.astype(BF16), preferred_element_type=F32)


def _bdot_nt(a, b):
    return lax.dot_general(a.astype(BF16), b.astype(BF16), (((1,), (1,)), ((), ())),
                           preferred_element_type=F32)


def _bdot_tn(a, b):
    return lax.dot_general(a.astype(BF16), b.astype(BF16), (((0,), (0,)), ((), ())),
                           preferred_element_type=F32)


def _split2(x):
    hi = x.astype(BF16)
    lo = (x - hi.astype(F32)).astype(BF16)
    return hi, lo


def _dot_exact_rhs(x, m):
    hi, lo = _split2(x)
    return (jnp.dot(hi, m, preferred_element_type=F32) + jnp.dot(lo, m, preferred_element_type=F32))


def _silu(x):
    return x * jax.nn.sigmoid(x)


def _mod_kernel(c_ref, w_ref, b_ref, o_ref):
    c = c_ref[...]
    o_ref[...] = jnp.dot(_silu(c), w_ref[...], precision=lax.Precision.HIGHEST,
                         preferred_element_type=F32) + b_ref[...]


def _modulation(cvec, w_ada, b_ada):
    rows, d = cvec.shape
    n = w_ada.shape[1]
    tn = 512
    return pl.pallas_call(
        _mod_kernel,
        out_shape=jax.ShapeDtypeStruct((rows, n), F32),
        grid=(n // tn,),
        in_specs=[pl.BlockSpec((rows, d), lambda j: (0, 0)),
                  pl.BlockSpec((d, tn), lambda j: (0, j)),
                  pl.BlockSpec((1, tn), lambda j: (0, j))],
        out_specs=pl.BlockSpec((rows, tn), lambda j: (0, j)),
        compiler_params=_cparams(("arbitrary",)),
        name="adaln_modulation",
    )(cvec, w_ada, b_ada.reshape(1, n))


def _inproj_kernel(x_ref, mod_ref, g_ref, w_ref, o_ref, *, tn):
    x = x_ref[0]
    y = x * lax.rsqrt(jnp.mean(x * x, axis=-1, keepdims=True) + RMS_EPS) * g_ref[...]
    m = mod_ref[0]
    h = (y * (1.0 + m[1:2]) + m[0:1]).astype(BF16)
    n = w_ref.shape[1]
    for j0 in range(0, n, tn):
        j1 = min(j0 + tn, n)
        o_ref[0, :, j0:j1] = jnp.dot(h, w_ref[:, j0:j1], preferred_element_type=F32)


def _inproj(x, mod, norm_g, w_bf16, tm):
    b, t, d = x.shape
    n = w_bf16.shape[1]
    return pl.pallas_call(
        functools.partial(_inproj_kernel, tn=2 * MXU_DIM),
        out_shape=jax.ShapeDtypeStruct((b, t, n), F32),
        grid=(b, t // tm),
        in_specs=[pl.BlockSpec((1, tm, d), lambda bi, i: (bi, i, 0)),
                  pl.BlockSpec((1, 8, d), lambda bi, i: (bi, 0, 0)),
                  pl.BlockSpec((1, d), lambda bi, i: (0, 0)),
                  pl.BlockSpec((d, n), lambda bi, i: (0, 0), pipeline_mode=pl.Buffered(1))],
        out_specs=pl.BlockSpec((1, tm, n), lambda bi, i: (bi, i, 0)),
        compiler_params=_cparams(("arbitrary", "arbitrary")),
        name="norm_mod_inproj",
    )(x, mod, norm_g.reshape(1, d), w_bf16)


_ROT = ((1.0, 0, -1.0, 1), (-1.0, 1, -1.0, 0), (-1.0, 0, 1.0, 1), (1.0, 1, 1.0, 0))


def _fourier_kernel(xf_ref, cc_ref, e_ref, o_ref, g_ref, *, n1, t2, rows):
    width = xf_ref.shape[2]

    @pl.when(pl.program_id(1) == 0)
    def _():
        def body(i, carry):
            r0 = pl.multiple_of(i * rows, rows)
            pq = []
            for a in range(n1):
                x = xf_ref[0, pl.ds(a * t2 + r0, rows), :]
                parts = [_bdot(x[:, g * FOURIER_GROUP:(g + 1) * FOURIER_GROUP], cc_ref[...])
                         for g in range(width // FOURIER_GROUP)]
                pq.append((jnp.concatenate([p[:, :FOURIER_GROUP] for p in parts], axis=1),
                           jnp.concatenate([p[:, FOURIER_GROUP:] for p in parts], axis=1)))
            for f1 in range(n1):
                re = im = None
                for a in range(n1):
                    sr, cr, si, ci = _ROT[(a * f1) % 4]
                    tr, ti = sr * pq[a][cr], si * pq[a][ci]
                    re = tr if re is None else re + tr
                    im = ti if im is None else im + ti
                g_ref[f1, pl.ds(r0, rows), :] = re.astype(BF16)
                g_ref[f1, pl.ds(t2 + r0, rows), :] = im.astype(BF16)
            return carry
        lax.fori_loop(0, t2 // rows, body, 0)

    for f1 in range(n1):
        o_ref[0, :, f1 * width:(f1 + 1) * width] = jnp.dot(e_ref[f1], g_ref[f1], preferred_element_type=F32)


def _dft_consts(t, n1):
    t2 = t // n1
    g = FOURIER_GROUP
    j = np.arange(g)
    ang = 2.0 * np.pi * ((j[:, None] * j[None, :]) % g) / g
    cc = jnp.asarray(np.concatenate([np.cos(ang), np.sin(ang)], axis=1) / math.sqrt(g), dtype=F32).astype(BF16)
    scale = 1.0 / math.sqrt(t)
    if n1 == 1:
        k = np.arange(t)
        ang = 2.0 * np.pi * ((k[:, None] * k[None, :]) % t) / t
        e = np.concatenate([np.cos(ang), np.sin(ang)], axis=1)[None] * scale
        return jnp.asarray(e, dtype=F32).astype(BF16), cc
    sa = 1 << (int(math.log2(t2)) // 2)
    sb = t2 // sa
    assert sa * sb == t2 and n1 * t2 == t
    tt = np.arange(t2)
    f1 = np.arange(n1)[:, None, None]
    bb = np.arange(sb)[None, :, None]
    ang_x = 2.0 * np.pi * ((tt[None, None, :] * (f1 + n1 * bb)) % t) / t
    aa = np.arange(sa)[:, None]
    ang_y = 2.0 * np.pi * ((tt[None, :] * aa) % sa) / sa
    xr, xi = jnp.asarray(np.cos(ang_x) * scale, F32), jnp.asarray(-np.sin(ang_x) * scale, F32)
    yr, yi = jnp.asarray(np.cos(ang_y), F32), jnp.asarray(-np.sin(ang_y), F32)
    xr, xi = xr[:, None], xi[:, None]
    yr, yi = yr[None, :, None], yi[None, :, None]
    re = (xr * yr - xi * yi).reshape(n1, t2, t2)
    im = (xr * yi + xi * yr).reshape(n1, t2, t2)
    return jnp.concatenate([re, -im], axis=2).astype(BF16), cc


def _fourier(u, t, width, n1, tf):
    b = u.shape[0]
    t2 = t // n1
    e, cc = _dft_consts(t, n1)
    kern = functools.partial(_fourier_kernel, n1=n1, t2=t2, rows=min(t2, 256))
    out = pl.pallas_call(
        kern,
        out_shape=jax.ShapeDtypeStruct((b, t2, n1 * width), F32),
        grid=(b, t2 // tf),
        in_specs=[pl.BlockSpec((1, t, width), lambda bi, i: (bi, 0, 0)),
                  pl.BlockSpec((FOURIER_GROUP, 2 * FOURIER_GROUP), lambda bi, i: (0, 0)),
                  pl.BlockSpec((n1, tf, 2 * t2), lambda bi, i: (0, i, 0))],
        out_specs=pl.BlockSpec((1, tf, n1 * width), lambda bi, i: (bi, i, 0)),
        scratch_shapes=[pltpu.VMEM((n1, 2 * t2, width), BF16)],
        compiler_params=_cparams(("arbitrary", "arbitrary")),
        name="fourier_mix",
    )(u, cc, e)
    return out.reshape(b, t, width)


def _shift(x, prev_halo, next_halo, t_total, row0, grid_shift):
    rows, lanes = x.shape
    row = lax.broadcasted_iota(jnp.int32, (rows, lanes), 0)
    lane = lax.broadcasted_iota(jnp.int32, (rows, lanes), 1)
    before = pltpu.roll(x, 1, 0)
    after = pltpu.roll(x, rows - 1, 0)
    if not grid_shift:
        prev = jnp.where(row == 0, 0.0, before)
        nxt = jnp.where(row == rows - 1, 0.0, after)
        return jnp.where((lane & 1) == 0, prev, nxt)
    col = row & (GRID_W - 1)
    grow = row + row0
    left = jnp.where(col == 0, 0.0, before)
    right = jnp.where(col == GRID_W - 1, 0.0, after)
    up = jnp.concatenate([prev_halo, x[:rows - GRID_W]], axis=0)
    up = jnp.where(grow < GRID_W, 0.0, up)
    down = jnp.concatenate([x[GRID_W:], next_halo], axis=0)
    down = jnp.where(grow >= t_total - GRID_W, 0.0, down)
    m = lane & 3
    return jnp.where(m == 0, left, jnp.where(m == 1, right, jnp.where(m == 2, up, down)))


def _prep_kernel(*refs, grid_shift, t_total, width, cw):
    if grid_shift:
        (r_ref, k_ref, v_ref, d_ref, rp_ref, kp_ref, vp_ref, dp_ref, rn_ref, kn_ref, vn_ref, dn_ref,
         mu_ref, mud_ref, kk_ref, wcat_ref, bias_ref, seg_ref,
         ro_ref, ko_ref, vo_ref, kko_ref, lwf_ref, lwb_ref, af_ref, ab_ref) = refs
    else:
        (r_ref, k_ref, v_ref, d_ref, mu_ref, mud_ref, kk_ref, wcat_ref, bias_ref, seg_ref,
         ro_ref, ko_ref, vo_ref, kko_ref, lwf_ref, lwb_ref, af_ref, ab_ref) = refs
        rp_ref = kp_ref = vp_ref = dp_ref = rn_ref = kn_ref = vn_ref = dn_ref = None
    rows = r_ref.shape[1]
    row0 = pl.program_id(1) * rows

    def lerp(x_ref, p_ref, n_ref, mu, sl):
        x = x_ref[0, :, sl]
        ph = p_ref[0, :, sl] if grid_shift else None
        nh = n_ref[0, :, sl] if grid_shift else None
        s = _shift(x, ph, nh, t_total, row0, grid_shift)
        return x + mu * (s - x)

    for j in range(width // cw):
        sl = slice(j * cw, (j + 1) * cw)
        ro_ref[0, :, sl] = lerp(r_ref, rp_ref, rn_ref, mu_ref[0:1, sl], sl)
        vo_ref[0, :, sl] = lerp(v_ref, vp_ref, vn_ref, mu_ref[2:3, sl], sl)
        k = lerp(k_ref, kp_ref, kn_ref, mu_ref[1:2, sl], sl)
        ko_ref[0, :, sl] = k
        kk = k * kk_ref[0:1, sl]
        ss = _dot_exact_rhs(kk * kk, seg_ref[...])
        kko_ref[0, :, sl] = kk / jnp.maximum(jnp.sqrt(ss), 1e-12)

    dsl = slice(0, LANES)
    d = lerp(d_ref, dp_ref, dn_ref, mud_ref[...], dsl)
    lane = lax.broadcasted_iota(jnp.int32, d.shape, 1)
    d = jnp.where(lane < HEAD, jnp.tanh(d), d).astype(BF16)
    outs = (lwf_ref, lwb_ref, af_ref, ab_ref)
    for z in range(4):
        for j in range(width // cw):
            sl = slice(j * cw, (j + 1) * cw)
            wsl = slice(z * width + j * cw, z * width + (j + 1) * cw)
            pre = jnp.dot(d, wcat_ref[:, wsl], preferred_element_type=F32) + bias_ref[z:z + 1, sl]
            sg = jax.nn.sigmoid(pre)
            outs[z][0, :, sl] = sg * (-math.exp(-0.5)) if z < 2 else sg


def _rwkv_prep(u, mu_shift, k_k, w_up, a_up, w0, a0, grid_shift, tt):
    b, t, _ = u.shape
    width = k_k.shape[0]
    cw = MXU_DIM
    mu = mu_shift[:3 * width].reshape(3, width)
    mud = mu_shift[3 * width:].reshape(1, LANES)
    zpad = jnp.zeros((HEAD, width), F32)
    wcat = jnp.concatenate([jnp.concatenate([w_up[0], zpad], 0), jnp.concatenate([w_up[1], zpad], 0),
                            jnp.concatenate([zpad, a_up[0]], 0), jnp.concatenate([zpad, a_up[1]], 0)],
                           axis=1).astype(BF16)
    bias = jnp.concatenate([w0, a0], axis=0)
    seg = _seg_matrix(cw)
    d_blk = (u.shape[2] - LANES) // LANES
    main = [pl.BlockSpec((1, tt, width), lambda bi, i, c=c: (bi, i, c)) for c in (1, 2, 3)]
    main.append(pl.BlockSpec((1, tt, LANES), lambda bi, i: (bi, i, d_blk)))
    ins = [u, u, u, u]
    specs = list(main)
    if grid_shift:
        hb = tt // GRID_W
        last = t // GRID_W - 1
        for off in (-1, hb):
            def imap(bi, i, c, off=off):
                return (bi, jnp.clip(i * hb + off, 0, last), c)
            specs += [pl.BlockSpec((1, GRID_W, width), functools.partial(imap, c=c)) for c in (1, 2, 3)]
            specs.append(pl.BlockSpec((1, GRID_W, LANES), functools.partial(imap, c=d_blk)))
            ins += [u, u, u, u]
    const = lambda shape: pl.BlockSpec(shape, lambda bi, i: (0, 0))
    specs += [const((3, width)), const((1, LANES)), const((1, width)), const((LANES, 4 * width)),
              const((4, width)), const((cw, cw))]
    ins += [mu, mud, k_k.reshape(1, width), wcat, bias, seg]
    out_spec = pl.BlockSpec((1, tt, width), lambda bi, i: (bi, i, 0))
    kern = functools.partial(_prep_kernel, grid_shift=grid_shift, t_total=t, width=width, cw=cw)
    return pl.pallas_call(
        kern,
        out_shape=[jax.ShapeDtypeStruct((b, t, width), F32)] * 8,
        grid=(b, t // tt),
        in_specs=specs,
        out_specs=[out_spec] * 8,
        compiler_params=_cparams(("arbitrary", "arbitrary")),
        name="rwkv_prep",
    )(*ins)


def _seg_matrix(n):
    i = np.arange(n) // HEAD
    return jnp.asarray((i[:, None] == i[None, :]).astype(np.float32), dtype=BF16)


def _scan_consts():
    t = np.arange(TBLK)
    same = (t[:, None] // CHUNK) == (t[None, :] // CHUNK)
    tri_f = same & (t[None, :] <= t[:, None])
    tri_b = same & (t[None, :] >= t[:, None])
    mk = lambda m: jnp.asarray(m.astype(np.float32), dtype=BF16)
    return mk(np.concatenate([tri_f, same], 0)), mk(np.concatenate([tri_b, same], 0))


def _heads_stacked(x, m0):
    return jnp.concatenate([jnp.where(m0, x, 0.0), jnp.where(m0, 0.0, x)], axis=0)


def _block_diag(x):
    n = x.shape[0]
    z = jnp.zeros((n, n), x.dtype)
    return jnp.concatenate([jnp.concatenate([x[:, :n], z], axis=1),
                            jnp.concatenate([z, x[:, n:]], axis=1)], axis=0)


def _scan_chain(in_refs, ls, k_a, cum, reverse, st_ref, st_idx, y_ref):
    r_ref, k_ref, v_ref, kk_ref, lw_ref, a_ref = in_refs
    n = TBLK
    lw = lw_ref[0, :, ls]
    hi, lo = _split2(lw)
    ct = jnp.dot(cum, jnp.concatenate([hi, lo], axis=1), preferred_element_type=F32)
    yield
    ct = ct[:, :LANES] + ct[:, LANES:]
    cl, tot = ct[:n], ct[n:]
    r, k, v, kk, a = r_ref[0, :, ls], k_ref[0, :, ls], v_ref[0, :, ls], kk_ref[0, :, ls], a_ref[0, :, ls]
    kd = k * (1.0 + (a - 1.0) * k_a)
    b = kk * a
    at = -kk * jnp.exp(cl - lw)
    rt = r * jnp.exp(cl)
    e_neg = jnp.exp(-cl)
    e_tail = jnp.exp(tot - cl)
    bh = b * e_tail
    kh = kd * e_tail
    w_chunk = jnp.exp(tot)

    lane2 = lax.broadcasted_iota(jnp.int32, (2 * n, LANES), 1)
    m0_2 = lane2 < HEAD
    m0 = lax.broadcasted_iota(jnp.int32, (n, LANES), 1) < HEAD
    lhs = jnp.concatenate([at, rt], axis=0)
    rhs = jnp.concatenate([b * e_neg, kd * e_neg], axis=0).astype(BF16)
    g0 = _bdot_nt(jnp.where(m0_2, lhs, 0.0), rhs)
    g1 = _bdot_nt(jnp.where(m0_2, 0.0, lhs), rhs)
    yield

    row = lax.broadcasted_iota(jnp.int32, (n, 2 * n), 0)
    col = lax.broadcasted_iota(jnp.int32, (n, 2 * n), 1) & (n - 1)
    same = (row // CHUNK) == (col // CHUNK)
    if reverse:
        strict, incl = same & (col > row), same & (col >= row)
    else:
        strict, incl = same & (col < row), same & (col <= row)
    pick = lambda rs, cs: jnp.concatenate([g0[rs, cs], g1[rs, cs]], axis=1)
    top, bot = slice(0, n), slice(n, 2 * n)
    a_ab = jnp.where(strict, pick(top, top), 0.0)
    a_ak = jnp.where(strict, pick(top, bot), 0.0).astype(BF16)
    a_rb = jnp.where(incl, pick(bot, top), 0.0).astype(BF16)
    a_rk = jnp.where(incl, pick(bot, bot), 0.0).astype(BF16)

    steps = int(math.log2(CHUNK)) - 1
    q = a_ab
    p = _bdot(a_ab, _block_diag(a_ab))
    yield
    for _ in range(steps - 1):
        res = _bdot(jnp.concatenate([p, q], axis=0), _block_diag(p))
        yield
        q = q + p + res[n:]
        p = res[:n]
    fin = _bdot(q, _block_diag(p))
    yield
    q16 = (q + p + fin).astype(BF16)

    v_s = _heads_stacked(v, m0).astype(BF16)
    qa = jnp.dot(q16, _heads_stacked(at, m0).astype(BF16), preferred_element_type=F32)
    akv = jnp.dot(a_ak, v_s, preferred_element_type=F32)
    rkv = jnp.dot(a_rk, v_s, preferred_element_type=F32)
    yield
    a_hat = at + qa
    uvq = jnp.dot(q16, _heads_stacked(akv, m0).astype(BF16), preferred_element_type=F32)
    yield
    uv = akv + uvq
    both = jnp.dot(a_rb, jnp.concatenate([_heads_stacked(a_hat, m0), _heads_stacked(uv, m0)],
                                         axis=1).astype(BF16), preferred_element_type=F32)
    yield
    r_hat = rt + both[:, :LANES]
    yv = both[:, LANES:] + rkv

    srow = lax.broadcasted_iota(jnp.int32, (LANES, LANES), 0)
    scol = lax.broadcasted_iota(jnp.int32, (LANES, LANES), 1)
    pair_diag = (srow < HEAD) == (scol < HEAD)
    state = st_ref[st_idx]
    n_chunks = n // CHUNK
    for c in (range(n_chunks - 1, -1, -1) if reverse else range(n_chunks)):
        sl = slice(c * CHUNK, (c + 1) * CHUNK)
        proj = _bdot_nt(jnp.concatenate([a_hat[sl], r_hat[sl]], axis=0), state)
        yield
        u_c = proj[:CHUNK] + uv[sl]
        y_ref[0, sl, ls] = proj[CHUNK:] + yv[sl]
        upd = _bdot_tn(jnp.concatenate([u_c, v[sl]], axis=0), jnp.concatenate([bh[sl], kh[sl]], axis=0))
        yield
        state = state * w_chunk[c * CHUNK:c * CHUNK + 1] + jnp.where(pair_diag, upd, 0.0)
    st_ref[st_idx] = state


def _run_interleaved(chains):
    chains = list(chains)
    while chains:
        alive = []
        for ch in chains:
            try:
                next(ch)
                alive.append(ch)
            except StopIteration:
                pass
        chains = alive


def _scan_kernel(*refs, has_init, want_state, groups):
    fwd_refs, bwd_refs = refs[:6], refs[6:12]
    ka_ref, cumf_ref, cumb_ref = refs[12:15]
    pos = 15
    s0_ref = None
    if has_init:
        s0_ref = refs[pos]
        pos += 1
    yf_ref, yb_ref = refs[pos:pos + 2]
    pos += 2
    so_ref = None
    if want_state:
        so_ref = refs[pos]
        pos += 1
    st_ref = refs[pos]
    i = pl.program_id(2)

    @pl.when(i == 0)
    def _():
        if has_init:
            st_ref[...] = s0_ref[0]
        else:
            st_ref[...] = jnp.zeros(st_ref.shape, F32)

    chains = []
    for g in range(groups):
        ls = slice(g * LANES, (g + 1) * LANES)
        k_a = ka_ref[:, ls]
        chains.append(_scan_chain(fwd_refs, ls, k_a, cumf_ref[...], False, st_ref, (0, g), yf_ref))
        chains.append(_scan_chain(bwd_refs, ls, k_a, cumb_ref[...], True, st_ref, (1, g), yb_ref))
    _run_interleaved(chains)

    if want_state:
        @pl.when(i == pl.num_programs(2) - 1)
        def _():
            so_ref[0] = st_ref[...]


def _rwkv_scan(r, k, v, kk, lw_f, lw_b, a_f, a_b, k_a, s0, want_state):
    b, t, width = r.shape
    pairs = width // LANES
    g = SCAN_PAIRS
    gw = g * LANES
    nt = t // TBLK
    fwd = pl.BlockSpec((1, TBLK, gw), lambda bi, p, i: (bi, i, p))
    bwd = pl.BlockSpec((1, TBLK, gw), lambda bi, p, i: (bi, nt - 1 - i, p))
    cmat = pl.BlockSpec((2 * TBLK, TBLK), lambda bi, p, i: (0, 0))
    st_spec = pl.BlockSpec((1, 2, g, LANES, LANES), lambda bi, p, i: (bi, 0, p, 0, 0))
    specs = [fwd] * 6 + [bwd] * 6 + [pl.BlockSpec((1, gw), lambda bi, p, i: (0, p)), cmat, cmat]
    ins = [r, k, v, kk, lw_f, a_f, r, k, v, kk, lw_b, a_b, k_a.reshape(1, width), *_scan_consts()]
    if s0 is not None:
        specs.append(st_spec)
        ins.append(s0)
    y_shape = jax.ShapeDtypeStruct((b, t, width), F32)
    out_shape = [y_shape, y_shape]
    out_specs = [fwd, bwd]
    if want_state:
        out_shape.append(jax.ShapeDtypeStruct((b, 2, pairs, LANES, LANES), F32))
        out_specs.append(st_spec)
    kern = functools.partial(_scan_kernel, has_init=s0 is not None, want_state=want_state, groups=g)
    return pl.pallas_call(
        kern,
        out_shape=out_shape,
        grid=(b, pairs // g, nt),
        in_specs=specs,
        out_specs=out_specs,
        scratch_shapes=[pltpu.VMEM((2, g, LANES, LANES), F32)],
        compiler_params=_cparams(("arbitrary", "arbitrary", "arbitrary")),
        name="rwkv7_scan",
    )(*ins)


def _pair_states(s):
    b, z, h, n, _ = s.shape
    sp = s.reshape(b, z, h // 2, 2, n, n)
    zero = jnp.zeros_like(sp[:, :, :, 0])
    top = jnp.concatenate([sp[:, :, :, 0], zero], axis=-1)
    bot = jnp.concatenate([zero, sp[:, :, :, 1]], axis=-1)
    return jnp.concatenate([top, bot], axis=-2)


def _unpair_states(sp):
    n = HEAD
    b, z, p = sp.shape[:3]
    return jnp.stack([sp[..., :n, :n], sp[..., n:, n:]], axis=3).reshape(b, z, 2 * p, n, n)


def _post_kernel(yf_ref, yb_ref, r_ref, k_ref, v_ref, gf_ref, gr_ref, g0_ref, g1_ref, fo_ref, x_ref, mod_ref,
                 vec_ref, seg_ref, wf_ref, wr_ref, wo_ref, o_ref, z_ref, *, cw):
    width = yf_ref.shape[2]
    inv_n = 1.0 / HEAD
    for j in range(width // cw):
        sl = slice(j * cw, (j + 1) * cw)
        seg = seg_ref[...]
        y = yf_ref[0, :, sl] + yb_ref[0, :, sl]
        mean = _dot_exact_rhs(y, seg) * inv_n
        d = y - mean
        var = _dot_exact_rhs(d * d, seg) * inv_n
        yn = d * lax.rsqrt(var + GN_EPS) * vec_ref[0:1, sl] + vec_ref[1:2, sl]
        rk = _dot_exact_rhs(r_ref[0, :, sl] * k_ref[0, :, sl] * vec_ref[2:3, sl], seg)
        z = (yn + rk * v_ref[0, :, sl]) * _silu(gr_ref[0, :, sl])
        z_ref[:, sl] = z.astype(BF16)
    out_r = jnp.dot(z_ref[...], wr_ref[...], preferred_element_type=F32)
    zf = (fo_ref[0] * _silu(gf_ref[0])).astype(BF16)
    out_f = jnp.dot(zf, wf_ref[...], preferred_element_type=F32)
    merged = jax.nn.sigmoid(g0_ref[0]) * out_f + jax.nn.sigmoid(g1_ref[0]) * out_r
    o = jnp.dot(merged.astype(BF16), wo_ref[...], preferred_element_type=F32)
    xn = x_ref[0] + mod_ref[0][2:3] * o
    o_ref[0] = xn * lax.rsqrt(jnp.mean(xn * xn, axis=-1, keepdims=True) + RMS_EPS) * vec_ref[3:4, :]


def _post(yf, yb, r, k, v, u, fo, x, mod, lnx_g, lnx_b, r_k, final_g, w_proj_f, w_proj_r, w_out, tm):
    b, t, d = x.shape
    width = yf.shape[2]
    fw = fo.shape[2]
    cw = MXU_DIM
    vec = jnp.stack([lnx_g, lnx_b, r_k.reshape(-1), final_g], axis=0)
    tok = lambda w, c: pl.BlockSpec((1, tm, w), lambda bi, i: (bi, i, c))
    const = lambda shape: pl.BlockSpec(shape, lambda bi, i: (0,) * len(shape))
    gr_off = 2 * fw + 3 * width
    mg_off = gr_off + width
    assert gr_off % width == 0 and mg_off % d == 0
    gr_blk, mg_blk = gr_off // width, mg_off // d
    specs = [tok(width, 0)] * 5 + [tok(fw, 1), tok(width, gr_blk), tok(d, mg_blk), tok(d, mg_blk + 1),
                                   tok(fw, 0), tok(d, 0), pl.BlockSpec((1, 8, d), lambda bi, i: (bi, 0, 0)),
                                   const((4, d)), const((cw, cw)), const((fw, d)), const((width, d)),
                                   const((d, d))]
    return pl.pallas_call(
        functools.partial(_post_kernel, cw=cw),
        out_shape=jax.ShapeDtypeStruct((b, t, d), F32),
        grid=(b, t // tm),
        in_specs=specs,
        out_specs=tok(d, 0),
        scratch_shapes=[pltpu.VMEM((tm, width), BF16)],
        compiler_params=_cparams(("arbitrary", "arbitrary")),
        name="post_mix",
    )(yf, yb, r, k, v, u, u, u, u, fo, x, mod, vec, _seg_matrix(cw),
      w_proj_f.astype(BF16), w_proj_r.astype(BF16), w_out.astype(BF16))


def _mixer_path(x, mod, s0, want_state, grid_shift, p, tiles):
    b, t, d = x.shape
    u = _inproj(x, mod, p["norm_g"], p["w_in"], tiles["inproj"])
    fo = _fourier(u, t, p["fourier_width"], tiles["fourier_n1"], tiles["fourier"])
    r, k, v, kk, lw_f, lw_b, a_f, a_b = _rwkv_prep(
        u, p["mu_shift"], p["k_k"], p["w_up"], p["a_up"], p["w0"], p["a0"], grid_shift, tiles["prep"])
    res = _rwkv_scan(r, k, v, kk, lw_f, lw_b, a_f, a_b, p["k_a"], s0, want_state)
    y = _post(res[0], res[1], r, k, v, u, fo, x, mod, p["lnx_g"], p["lnx_b"], p["r_k"], p["final_g"],
              p["w_proj_f"], p["w_proj_r"], p["w_out"], tiles["post"])
    return y, (res[2] if want_state else None)


def _tiles(t, grid_shift):
    if not grid_shift:
        return dict(inproj=t, fourier=t, fourier_n1=1, prep=t, post=t)
    return dict(inproj=256, fourier=256, fourier_n1=4, prep=256, post=256)


def kernel(x_prompt, x_sample, state_rwkv, c, c_ctx, norm_g, w_ada, b_ada, w_in, mu_shift, w0, w_up, a0,
           a_up, k_k, k_a, r_k, lnx_g, lnx_b, w_proj_f, w_proj_r, w_out, final_g):
    depth = w_in.shape[0]
    assert depth == 1, "the final norm is fused into the single layer's post kernel"
    bp, tp, d = x_prompt.shape
    bs, ts, _ = x_sample.shape
    width = k_k.shape[1]
    fw = w_proj_f.shape[1]
    l = 0
    sh_end = 2 * fw + 3 * width
    rank2 = mu_shift.shape[1] - 3 * width
    w_in_l = jnp.concatenate([w_in[l][:, :sh_end], w_in[l][:, sh_end + rank2:],
                              w_in[l][:, sh_end:sh_end + rank2]], axis=1).astype(BF16)
    p = dict(norm_g=norm_g[l], w_in=w_in_l, fourier_width=fw, mu_shift=mu_shift[l], k_k=k_k[l], w_up=w_up[l],
             a_up=a_up[l], w0=w0[l], a0=a0[l], k_a=k_a[l], lnx_g=lnx_g[l], lnx_b=lnx_b[l], r_k=r_k[l],
             final_g=final_g, w_proj_f=w_proj_f[l], w_proj_r=w_proj_r[l], w_out=w_out[l])

    rows = 8
    cvec = jnp.concatenate([c_ctx[None], c, jnp.zeros((rows - 1 - bs, d), F32)], axis=0)
    m = _modulation(cvec, w_ada[l], b_ada[l])
    m3 = m.reshape(rows, 3, d)
    mod = jnp.concatenate([m3, jnp.zeros((rows, 5, d), F32)], axis=1)

    yp, sp = _mixer_path(x_prompt, jnp.broadcast_to(mod[0:1], (bp, 8, d)), None, True, False, p,
                         _tiles(tp, False))
    s0 = _pair_states(state_rwkv[:, l])
    ys, _ = _mixer_path(x_sample, mod[1:1 + bs], s0, False, True, p, _tiles(ts, True))
    new_state = _unpair_states(sp)[:, None]
    return yp, ys, new_state
```

```python
import functools
import math

import numpy as np
import jax
import jax.numpy as jnp
from jax import lax
from jax.experimental import pallas as pl
from jax.experimental.pallas import tpu as pltpu

F32 = jnp.float32
BF16 = jnp.bfloat16

LANES = 128
MXU_DIM = 256
VMEM_LIMIT = 56 * 1024 * 1024

HEAD = 64
GRID_W = 64
FOURIER_GROUP = 128
RMS_EPS = 1e-6
GN_EPS = 64e-5

TBLK = 128
CHUNK = 32
SCAN_PAIRS = 4


def _cparams(sem):
    return pltpu.CompilerParams(dimension_semantics=sem, vmem_limit_bytes=VMEM_LIMIT)


def _bdot(a, b):
    return jnp.dot(a.astype(BF16), b.astype(BF16), preferred_element_type=F32)


def _bdot_nt(a, b):
    return lax.dot_general(a.astype(BF16), b.astype(BF16), (((1,), (1,)), ((), ())),
                           preferred_element_type=F32)


def _bdot_tn(a, b):
    return lax.dot_general(a.astype(BF16), b.astype(BF16), (((0,), (0,)), ((), ())),
                           preferred_element_type=F32)


def _split2(x):
    hi = x.astype(BF16)
    lo = (x - hi.astype(F32)).astype(BF16)
    return hi, lo


def _dot_exact_rhs(x, m):
    hi, lo = _split2(x)
    return (jnp.dot(hi, m, preferred_element_type=F32) + jnp.dot(lo, m, preferred_element_type=F32))


def _silu(x):
    return x * jax.nn.sigmoid(x)


def _mod_kernel(c_ref, w_ref, b_ref, o_ref):
    c = c_ref[...]
    o_ref[...] = jnp.dot(_silu(c), w_ref[...], precision=lax.Precision.HIGHEST,
                         preferred_element_type=F32) + b_ref[...]


def _modulation(cvec, w_ada, b_ada):
    rows, d = cvec.shape
    n = w_ada.shape[1]
    tn = 512
    return pl.pallas_call(
        _mod_kernel,
        out_shape=jax.ShapeDtypeStruct((rows, n), F32),
        grid=(n // tn,),
        in_specs=[pl.BlockSpec((rows, d), lambda j: (0, 0)),
                  pl.BlockSpec((d, tn), lambda j: (0, j)),
                  pl.BlockSpec((1, tn), lambda j: (0, j))],
        out_specs=pl.BlockSpec((rows, tn), lambda j: (0, j)),
        compiler_params=_cparams(("arbitrary",)),
        name="adaln_modulation",
    )(cvec, w_ada, b_ada.reshape(1, n))


def _inproj_kernel(x_ref, mod_ref, g_ref, w_ref, o_ref, *, tn, segments):
    x = x_ref[0]
    y = x * lax.rsqrt(jnp.mean(x * x, axis=-1, keepdims=True) + RMS_EPS) * g_ref[...]
    m = mod_ref[0]
    h = (y * (1.0 + m[1:2]) + m[0:1]).astype(BF16)
    for src, dst, size in segments:
        for j0 in range(0, size, tn):
            w = min(tn, size - j0)
            o_ref[0, :, dst + j0:dst + j0 + w] = jnp.dot(h, w_ref[:, src + j0:src + j0 + w],
                                                         preferred_element_type=F32)


def _inproj(x, mod, norm_g, w_bf16, tm, segments):
    b, t, d = x.shape
    n = w_bf16.shape[1]
    return pl.pallas_call(
        functools.partial(_inproj_kernel, tn=2 * MXU_DIM, segments=segments),
        out_shape=jax.ShapeDtypeStruct((b, t, n), F32),
        grid=(b, t // tm),
        in_specs=[pl.BlockSpec((1, tm, d), lambda bi, i: (bi, i, 0)),
                  pl.BlockSpec((1, 8, d), lambda bi, i: (bi, 0, 0)),
                  pl.BlockSpec((1, d), lambda bi, i: (0, 0)),
                  pl.BlockSpec((d, n), lambda bi, i: (0, 0), pipeline_mode=pl.Buffered(1))],
        out_specs=pl.BlockSpec((1, tm, n), lambda bi, i: (bi, i, 0)),
        compiler_params=_cparams(("arbitrary", "arbitrary")),
        name="norm_mod_inproj",
    )(x, mod, norm_g.reshape(1, d), w_bf16)


_ROT = ((1.0, 0, -1.0, 1), (-1.0, 1, -1.0, 0), (-1.0, 0, 1.0, 1), (1.0, 1, 1.0, 0))


def _fourier_kernel(xf_ref, cc_ref, e_ref, o_ref, g_ref, il_ref, *, n1, t2, rows):
    width = xf_ref.shape[2]

    @pl.when(pl.program_id(1) == 0)
    def _():
        def body(i, carry):
            r0 = pl.multiple_of(i * rows, rows)
            pq = []
            for a in range(n1):
                x = xf_ref[0, pl.ds(a * t2 + r0, rows), :]
                parts = [_bdot(x[:, g * FOURIER_GROUP:(g + 1) * FOURIER_GROUP], cc_ref[...])
                         for g in range(width // FOURIER_GROUP)]
                pq.append((jnp.concatenate([p[:, :FOURIER_GROUP] for p in parts], axis=1),
                           jnp.concatenate([p[:, FOURIER_GROUP:] for p in parts], axis=1)))
            for f1 in range(n1):
                re = im = None
                for a in range(n1):
                    sr, cr, si, ci = _ROT[(a * f1) % 4]
                    tr, ti = sr * pq[a][cr], si * pq[a][ci]
                    re = tr if re is None else re + tr
                    im = ti if im is None else im + ti
                g_ref[f1, pl.ds(r0, rows), :] = re.astype(BF16)
                g_ref[f1, pl.ds(t2 + r0, rows), :] = im.astype(BF16)
            return carry
        lax.fori_loop(0, t2 // rows, body, 0)

    tf = e_ref.shape[1]
    for f1 in range(n1):
        res = jnp.dot(e_ref[f1], g_ref[f1], preferred_element_type=F32)
        if n1 == 1:
            o_ref[0] = res
        else:
            for g in range(width // LANES):
                il_ref[g, pl.ds(f1, tf, stride=n1), :] = res[:, g * LANES:(g + 1) * LANES]
    if n1 > 1:
        for g in range(width // LANES):
            o_ref[0, :, g * LANES:(g + 1) * LANES] = il_ref[g]


def _dft_consts(t, n1):
    t2 = t // n1
    g = FOURIER_GROUP
    j = np.arange(g)
    ang = 2.0 * np.pi * ((j[:, None] * j[None, :]) % g) / g
    cc = jnp.asarray(np.concatenate([np.cos(ang), np.sin(ang)], axis=1) / math.sqrt(g), dtype=F32).astype(BF16)
    scale = 1.0 / math.sqrt(t)
    if n1 == 1:
        k = np.arange(t)
        ang = 2.0 * np.pi * ((k[:, None] * k[None, :]) % t) / t
        e = np.concatenate([np.cos(ang), np.sin(ang)], axis=1)[None] * scale
        return jnp.asarray(e, dtype=F32).astype(BF16), cc
    sa = 1 << (int(math.log2(t2)) // 2)
    sb = t2 // sa
    assert sa * sb == t2 and n1 * t2 == t
    tt = np.arange(t2)
    f1 = np.arange(n1)[:, None, None]
    bb = np.arange(sb)[None, :, None]
    ang_x = 2.0 * np.pi * ((tt[None, None, :] * (f1 + n1 * bb)) % t) / t
    aa = np.arange(sa)[:, None]
    ang_y = 2.0 * np.pi * ((tt[None, :] * aa) % sa) / sa
    xr, xi = jnp.asarray(np.cos(ang_x) * scale, F32), jnp.asarray(-np.sin(ang_x) * scale, F32)
    yr, yi = jnp.asarray(np.cos(ang_y), F32), jnp.asarray(-np.sin(ang_y), F32)
    xr, xi = xr[:, None], xi[:, None]
    yr, yi = yr[None, :, None], yi[None, :, None]
    re = (xr * yr - xi * yi).reshape(n1, t2, t2)
    im = (xr * yi + xi * yr).reshape(n1, t2, t2)
    return jnp.concatenate([re, -im], axis=2).astype(BF16), cc


def _fourier(u, t, width, n1, tf):
    b = u.shape[0]
    t2 = t // n1
    e, cc = _dft_consts(t, n1)
    kern = functools.partial(_fourier_kernel, n1=n1, t2=t2, rows=min(t2, 256))
    return pl.pallas_call(
        kern,
        out_shape=jax.ShapeDtypeStruct((b, t, width), F32),
        grid=(b, t2 // tf),
        in_specs=[pl.BlockSpec((1, t, width), lambda bi, i: (bi, 0, 0)),
                  pl.BlockSpec((FOURIER_GROUP, 2 * FOURIER_GROUP), lambda bi, i: (0, 0)),
                  pl.BlockSpec((n1, tf, 2 * t2), lambda bi, i: (0, i, 0))],
        out_specs=pl.BlockSpec((1, n1 * tf, width), lambda bi, i: (bi, i, 0)),
        scratch_shapes=[pltpu.VMEM((n1, 2 * t2, width), BF16),
                        pltpu.VMEM((width // LANES, n1 * tf if n1 > 1 else 8, LANES), F32)],
        compiler_params=_cparams(("arbitrary", "arbitrary")),
        name="fourier_mix",
    )(u, cc, e)


def _shift(x, prev_halo, next_halo, t_total, row0, grid_shift):
    rows, lanes = x.shape
    row = lax.broadcasted_iota(jnp.int32, (rows, lanes), 0)
    lane = lax.broadcasted_iota(jnp.int32, (rows, lanes), 1)
    before = pltpu.roll(x, 1, 0)
    after = pltpu.roll(x, rows - 1, 0)
    if not grid_shift:
        prev = jnp.where(row == 0, 0.0, before)
        nxt = jnp.where(row == rows - 1, 0.0, after)
        return jnp.where((lane & 1) == 0, prev, nxt)
    col = row & (GRID_W - 1)
    grow = row + row0
    left = jnp.where(col == 0, 0.0, before)
    right = jnp.where(col == GRID_W - 1, 0.0, after)
    up = jnp.concatenate([prev_halo, x[:rows - GRID_W]], axis=0)
    up = jnp.where(grow < GRID_W, 0.0, up)
    down = jnp.concatenate([x[GRID_W:], next_halo], axis=0)
    down = jnp.where(grow >= t_total - GRID_W, 0.0, down)
    m = lane & 3
    return jnp.where(m == 0, left, jnp.where(m == 1, right, jnp.where(m == 2, up, down)))


def _prep_kernel(*refs, grid_shift, t_total, width, cw):
    if grid_shift:
        (r_ref, k_ref, v_ref, d_ref, rp_ref, kp_ref, vp_ref, dp_ref, rn_ref, kn_ref, vn_ref, dn_ref,
         mu_ref, mud_ref, kk_ref, wcat_ref, bias_ref, seg_ref,
         ro_ref, ko_ref, vo_ref, kko_ref, lwf_ref, lwb_ref, af_ref, ab_ref) = refs
    else:
        (r_ref, k_ref, v_ref, d_ref, mu_ref, mud_ref, kk_ref, wcat_ref, bias_ref, seg_ref,
         ro_ref, ko_ref, vo_ref, kko_ref, lwf_ref, lwb_ref, af_ref, ab_ref) = refs
        rp_ref = kp_ref = vp_ref = dp_ref = rn_ref = kn_ref = vn_ref = dn_ref = None
    rows = r_ref.shape[1]
    row0 = pl.program_id(1) * rows

    def lerp(x_ref, p_ref, n_ref, mu, sl):
        x = x_ref[0, :, sl]
        ph = p_ref[0, :, sl] if grid_shift else None
        nh = n_ref[0, :, sl] if grid_shift else None
        s = _shift(x, ph, nh, t_total, row0, grid_shift)
        return x + mu * (s - x)

    for j in range(width // cw):
        sl = slice(j * cw, (j + 1) * cw)
        ro_ref[0, :, sl] = lerp(r_ref, rp_ref, rn_ref, mu_ref[0:1, sl], sl)
        vo_ref[0, :, sl] = lerp(v_ref, vp_ref, vn_ref, mu_ref[2:3, sl], sl)
        k = lerp(k_ref, kp_ref, kn_ref, mu_ref[1:2, sl], sl)
        ko_ref[0, :, sl] = k
        kk = k * kk_ref[0:1, sl]
        ss = _dot_exact_rhs(kk * kk, seg_ref[...])
        kko_ref[0, :, sl] = kk / jnp.maximum(jnp.sqrt(ss), 1e-12)

    dsl = slice(0, LANES)
    d = lerp(d_ref, dp_ref, dn_ref, mud_ref[...], dsl)
    lane = lax.broadcasted_iota(jnp.int32, d.shape, 1)
    d = jnp.where(lane < HEAD, jnp.tanh(d), d).astype(BF16)
    outs = (lwf_ref, lwb_ref, af_ref, ab_ref)
    for z in range(4):
        for j in range(width // cw):
            sl = slice(j * cw, (j + 1) * cw)
            wsl = slice(z * width + j * cw, z * width + (j + 1) * cw)
            pre = jnp.dot(d, wcat_ref[:, wsl], preferred_element_type=F32) + bias_ref[z:z + 1, sl]
            sg = jax.nn.sigmoid(pre)
            outs[z][0, :, sl] = sg * (-math.exp(-0.5)) if z < 2 else sg


def _rwkv_prep(u, mu_shift, k_k, w_up, a_up, w0, a0, grid_shift, tt):
    b, t, _ = u.shape
    width = k_k.shape[0]
    cw = MXU_DIM
    mu = mu_shift[:3 * width].reshape(3, width)
    mud = mu_shift[3 * width:].reshape(1, LANES)
    zpad = jnp.zeros((HEAD, width), F32)
    wcat = jnp.concatenate([jnp.concatenate([w_up[0], zpad], 0), jnp.concatenate([w_up[1], zpad], 0),
                            jnp.concatenate([zpad, a_up[0]], 0), jnp.concatenate([zpad, a_up[1]], 0)],
                           axis=1).astype(BF16)
    bias = jnp.concatenate([w0, a0], axis=0)
    seg = _seg_matrix(cw)
    d_blk = (u.shape[2] - LANES) // LANES
    main = [pl.BlockSpec((1, tt, width), lambda bi, i, c=c: (bi, i, c)) for c in (1, 2, 3)]
    main.append(pl.BlockSpec((1, tt, LANES), lambda bi, i: (bi, i, d_blk)))
    ins = [u, u, u, u]
    specs = list(main)
    if grid_shift:
        hb = tt // GRID_W
        last = t // GRID_W - 1
        for off in (-1, hb):
            def imap(bi, i, c, off=off):
                return (bi, jnp.clip(i * hb + off, 0, last), c)
            specs += [pl.BlockSpec((1, GRID_W, width), functools.partial(imap, c=c)) for c in (1, 2, 3)]
            specs.append(pl.BlockSpec((1, GRID_W, LANES), functools.partial(imap, c=d_blk)))
            ins += [u, u, u, u]
    const = lambda shape: pl.BlockSpec(shape, lambda bi, i: (0, 0))
    specs += [const((3, width)), const((1, LANES)), const((1, width)), const((LANES, 4 * width)),
              const((4, width)), const((cw, cw))]
    ins += [mu, mud, k_k.reshape(1, width), wcat, bias, seg]
    out_spec = pl.BlockSpec((1, tt, width), lambda bi, i: (bi, i, 0))
    kern = functools.partial(_prep_kernel, grid_shift=grid_shift, t_total=t, width=width, cw=cw)
    return pl.pallas_call(
        kern,
        out_shape=[jax.ShapeDtypeStruct((b, t, width), F32)] * 8,
        grid=(b, t // tt),
        in_specs=specs,
        out_specs=[out_spec] * 8,
        compiler_params=_cparams(("arbitrary", "arbitrary")),
        name="rwkv_prep",
    )(*ins)


def _seg_matrix(n):
    i = np.arange(n) // HEAD
    return jnp.asarray((i[:, None] == i[None, :]).astype(np.float32), dtype=BF16)


def _scan_consts():
    t = np.arange(TBLK)
    same = (t[:, None] // CHUNK) == (t[None, :] // CHUNK)
    tri_f = same & (t[None, :] <= t[:, None])
    tri_b = same & (t[None, :] >= t[:, None])
    mk = lambda m: jnp.asarray(m.astype(np.float32), dtype=BF16)
    return mk(tri_f), mk(tri_b)


def _heads_stacked(x, m0):
    return jnp.concatenate([jnp.where(m0, x, 0.0), jnp.where(m0, 0.0, x)], axis=0)


def _scan_chain(in_refs, ls, k_a, cum, reverse, st_ref, st_idx, y_ref):
    r_ref, k_ref, v_ref, kk_ref, lw_ref, a_ref = in_refs
    n = TBLK
    lw = lw_ref[0, :, ls]
    hi, lo = _split2(lw)
    ct = jnp.dot(cum, jnp.concatenate([hi, lo], axis=1), preferred_element_type=F32)
    yield
    cl = ct[:, :LANES] + ct[:, LANES:]
    n_chunks = n // CHUNK
    ends = [c * CHUNK if reverse else (c + 1) * CHUNK - 1 for c in range(n_chunks)]
    tot_rows = [cl[e:e + 1] for e in ends]
    tot = jnp.concatenate([jnp.broadcast_to(tr, (CHUNK, LANES)) for tr in tot_rows], axis=0)
    r, k, v, kk, a = r_ref[0, :, ls], k_ref[0, :, ls], v_ref[0, :, ls], kk_ref[0, :, ls], a_ref[0, :, ls]
    kd = k * (1.0 + (a - 1.0) * k_a)
    b = kk * a
    at = -kk * jnp.exp(cl - lw)
    rt = r * jnp.exp(cl)
    e_neg = jnp.exp(-cl)
    e_tail = jnp.exp(tot - cl)
    bh = b * e_tail
    kh = kd * e_tail
    w_chunk = [jnp.exp(tr) for tr in tot_rows]

    m0 = lax.broadcasted_iota(jnp.int32, (n, LANES), 1) < HEAD
    lhs = jnp.concatenate([_heads_stacked(at, m0), _heads_stacked(rt, m0)], axis=0)
    rhs = jnp.concatenate([b * e_neg, kd * e_neg], axis=0)
    g = _bdot_nt(lhs, rhs)
    yield

    row = lax.broadcasted_iota(jnp.int32, (n, 2 * n), 0)
    col = lax.broadcasted_iota(jnp.int32, (n, 2 * n), 1) & (n - 1)
    same = (row // CHUNK) == (col // CHUNK)
    if reverse:
        strict, incl = same & (col > row), same & (col >= row)
    else:
        strict, incl = same & (col < row), same & (col <= row)
    pick = lambda r0, cs: jnp.concatenate([g[r0:r0 + n, cs], g[r0 + n:r0 + 2 * n, cs]], axis=1)
    left, right = slice(0, n), slice(n, 2 * n)
    a_ab = jnp.where(strict, pick(0, left), 0.0)
    a_kr = jnp.concatenate([jnp.where(strict, pick(0, right), 0.0),
                            jnp.where(incl, pick(2 * n, right), 0.0)], axis=0).astype(BF16)
    a_rb = jnp.where(incl, pick(2 * n, left), 0.0).astype(BF16)

    blocks = 2 * n_chunks
    brow = lax.broadcasted_iota(jnp.int32, (2 * n, 2 * n), 0)
    bcol = lax.broadcasted_iota(jnp.int32, (2 * n, 2 * n), 1)
    on_block = (brow // CHUNK) == (bcol // CHUNK)
    spread = lambda x: jnp.where(on_block, jnp.concatenate([x.astype(BF16)] * blocks, axis=0), 0.0)
    steps = int(math.log2(CHUNK)) - 1
    q = a_ab[:CHUNK]
    for c in range(1, n_chunks):
        q = q + a_ab[c * CHUNK:(c + 1) * CHUNK]
    p = _bdot(q, spread(q))
    yield
    for _ in range(steps - 1):
        res = _bdot(jnp.concatenate([p, q], axis=0), spread(p))
        yield
        q = q + p + res[CHUNK:]
        p = res[:CHUNK]
    fin = _bdot(q, spread(p))
    yield
    q = q + p + fin
    q16 = jnp.where(same, jnp.concatenate([q] * n_chunks, axis=0), 0.0).astype(BF16)

    v_s = _heads_stacked(v, m0).astype(BF16)
    qa = jnp.dot(q16, _heads_stacked(at, m0).astype(BF16), preferred_element_type=F32)
    kr = jnp.dot(a_kr, v_s, preferred_element_type=F32)
    akv, rkv = kr[:n], kr[n:]
    yield
    a_hat = at + qa
    uvq = jnp.dot(q16, _heads_stacked(akv, m0).astype(BF16), preferred_element_type=F32)
    yield
    uv = akv + uvq
    both = jnp.dot(a_rb, jnp.concatenate([_heads_stacked(a_hat, m0), _heads_stacked(uv, m0)],
                                         axis=1).astype(BF16), preferred_element_type=F32)
    yield
    r_hat = rt + both[:, :LANES]
    yv = both[:, LANES:] + rkv

    srow = lax.broadcasted_iota(jnp.int32, (LANES, LANES), 0)
    scol = lax.broadcasted_iota(jnp.int32, (LANES, LANES), 1)
    pair_diag = (srow < HEAD) == (scol < HEAD)
    state = st_ref[st_idx]
    for c in (range(n_chunks - 1, -1, -1) if reverse else range(n_chunks)):
        sl = slice(c * CHUNK, (c + 1) * CHUNK)
        proj = _bdot_nt(jnp.concatenate([a_hat[sl], r_hat[sl]], axis=0), state)
        yield
        u_c = proj[:CHUNK] + uv[sl]
        y_ref[0, sl, ls] = proj[CHUNK:] + yv[sl]
        upd = _bdot_tn(jnp.concatenate([u_c, v[sl]], axis=0), jnp.concatenate([bh[sl], kh[sl]], axis=0))
        yield
        state = state * w_chunk[c] + jnp.where(pair_diag, upd, 0.0)
    st_ref[st_idx] = state


def _run_interleaved(chains):
    chains = list(chains)
    while chains:
        alive = []
        for ch in chains:
            try:
                next(ch)
                alive.append(ch)
            except StopIteration:
                pass
        chains = alive


def _scan_kernel(*refs, has_init, want_state, groups):
    fwd_refs, bwd_refs = refs[:6], refs[6:12]
    ka_ref, cumf_ref, cumb_ref = refs[12:15]
    pos = 15
    s0_ref = None
    if has_init:
        s0_ref = refs[pos]
        pos += 1
    yf_ref, yb_ref = refs[pos:pos + 2]
    pos += 2
    so_ref = None
    if want_state:
        so_ref = refs[pos]
        pos += 1
    st_ref = refs[pos]
    i = pl.program_id(2)

    @pl.when(i == 0)
    def _():
        if has_init:
            st_ref[...] = s0_ref[0]
        else:
            st_ref[...] = jnp.zeros(st_ref.shape, F32)

    chains = []
    for g in range(groups):
        ls = slice(g * LANES, (g + 1) * LANES)
        k_a = ka_ref[:, ls]
        chains.append(_scan_chain(fwd_refs, ls, k_a, cumf_ref[...], False, st_ref, (0, g), yf_ref))
        chains.append(_scan_chain(bwd_refs, ls, k_a, cumb_ref[...], True, st_ref, (1, g), yb_ref))
    _run_interleaved(chains)

    if want_state:
        @pl.when(i == pl.num_programs(2) - 1)
        def _():
            so_ref[0] = st_ref[...]


def _rwkv_scan(r, k, v, kk, lw_f, lw_b, a_f, a_b, k_a, s0, want_state):
    b, t, width = r.shape
    pairs = width // LANES
    g = SCAN_PAIRS
    gw = g * LANES
    nt = t // TBLK
    fwd = pl.BlockSpec((1, TBLK, gw), lambda bi, p, i: (bi, i, p))
    bwd = pl.BlockSpec((1, TBLK, gw), lambda bi, p, i: (bi, nt - 1 - i, p))
    cmat = pl.BlockSpec((TBLK, TBLK), lambda bi, p, i: (0, 0))
    st_spec = pl.BlockSpec((1, 2, g, LANES, LANES), lambda bi, p, i: (bi, 0, p, 0, 0))
    specs = [fwd] * 6 + [bwd] * 6 + [pl.BlockSpec((1, gw), lambda bi, p, i: (0, p)), cmat, cmat]
    ins = [r, k, v, kk, lw_f, a_f, r, k, v, kk, lw_b, a_b, k_a.reshape(1, width), *_scan_consts()]
    if s0 is not None:
        specs.append(st_spec)
        ins.append(s0)
    y_shape = jax.ShapeDtypeStruct((b, t, width), F32)
    out_shape = [y_shape, y_shape]
    out_specs = [fwd, bwd]
    if want_state:
        out_shape.append(jax.ShapeDtypeStruct((b, 2, pairs, LANES, LANES), F32))
        out_specs.append(st_spec)
    kern = functools.partial(_scan_kernel, has_init=s0 is not None, want_state=want_state, groups=g)
    return pl.pallas_call(
        kern,
        out_shape=out_shape,
        grid=(b, pairs // g, nt),
        in_specs=specs,
        out_specs=out_specs,
        scratch_shapes=[pltpu.VMEM((2, g, LANES, LANES), F32)],
        compiler_params=_cparams(("arbitrary", "arbitrary", "arbitrary")),
        name="rwkv7_scan",
    )(*ins)


def _pair_states(s):
    b, z, h, n, _ = s.shape
    sp = s.reshape(b, z, h // 2, 2, n, n)
    zero = jnp.zeros_like(sp[:, :, :, 0])
    top = jnp.concatenate([sp[:, :, :, 0], zero], axis=-1)
    bot = jnp.concatenate([zero, sp[:, :, :, 1]], axis=-1)
    return jnp.concatenate([top, bot], axis=-2)


def _unpair_states(sp):
    n = HEAD
    b, z, p = sp.shape[:3]
    return jnp.stack([sp[..., :n, :n], sp[..., n:, n:]], axis=3).reshape(b, z, 2 * p, n, n)


def _post_kernel(yf_ref, yb_ref, r_ref, k_ref, v_ref, gf_ref, gr_ref, g0_ref, g1_ref, fo_ref, x_ref, mod_ref,
                 vec_ref, seg_ref, wf_ref, wr_ref, wo_ref, o_ref, z_ref, *, cw):
    width = yf_ref.shape[2]
    inv_n = 1.0 / HEAD
    for j in range(width // cw):
        sl = slice(j * cw, (j + 1) * cw)
        seg = seg_ref[...]
        y = yf_ref[0, :, sl] + yb_ref[0, :, sl]
        mean = _dot_exact_rhs(y, seg) * inv_n
        d = y - mean
        var = _dot_exact_rhs(d * d, seg) * inv_n
        yn = d * lax.rsqrt(var + GN_EPS) * vec_ref[0:1, sl] + vec_ref[1:2, sl]
        rk = _dot_exact_rhs(r_ref[0, :, sl] * k_ref[0, :, sl] * vec_ref[2:3, sl], seg)
        z = (yn + rk * v_ref[0, :, sl]) * _silu(gr_ref[0, :, sl])
        z_ref[:, sl] = z.astype(BF16)
    out_r = jnp.dot(z_ref[...], wr_ref[...], preferred_element_type=F32)
    zf = (fo_ref[0] * _silu(gf_ref[0])).astype(BF16)
    out_f = jnp.dot(zf, wf_ref[...], preferred_element_type=F32)
    merged = jax.nn.sigmoid(g0_ref[0]) * out_f + jax.nn.sigmoid(g1_ref[0]) * out_r
    o = jnp.dot(merged.astype(BF16), wo_ref[...], preferred_element_type=F32)
    xn = x_ref[0] + mod_ref[0][2:3] * o
    o_ref[0] = xn * lax.rsqrt(jnp.mean(xn * xn, axis=-1, keepdims=True) + RMS_EPS) * vec_ref[3:4, :]


def _post(yf, yb, r, k, v, u, fo, x, mod, lnx_g, lnx_b, r_k, final_g, w_proj_f, w_proj_r, w_out, tm):
    b, t, d = x.shape
    width = yf.shape[2]
    fw = fo.shape[2]
    cw = MXU_DIM
    vec = jnp.stack([lnx_g, lnx_b, r_k.reshape(-1), final_g], axis=0)
    tok = lambda w, c: pl.BlockSpec((1, tm, w), lambda bi, i: (bi, i, c))
    const = lambda shape: pl.BlockSpec(shape, lambda bi, i: (0,) * len(shape))
    gr_off = 2 * fw + 3 * width
    mg_off = gr_off + width
    assert gr_off % width == 0 and mg_off % d == 0
    gr_blk, mg_blk = gr_off // width, mg_off // d
    specs = [tok(width, 0)] * 5 + [tok(fw, 1), tok(width, gr_blk), tok(d, mg_blk), tok(d, mg_blk + 1),
                                   tok(fw, 0), tok(d, 0), pl.BlockSpec((1, 8, d), lambda bi, i: (bi, 0, 0)),
                                   const((4, d)), const((cw, cw)), const((fw, d)), const((width, d)),
                                   const((d, d))]
    return pl.pallas_call(
        functools.partial(_post_kernel, cw=cw),
        out_shape=jax.ShapeDtypeStruct((b, t, d), F32),
        grid=(b, t // tm),
        in_specs=specs,
        out_specs=tok(d, 0),
        scratch_shapes=[pltpu.VMEM((tm, width), BF16)],
        compiler_params=_cparams(("arbitrary", "arbitrary")),
        name="post_mix",
    )(yf, yb, r, k, v, u, u, u, u, fo, x, mod, vec, _seg_matrix(cw),
      w_proj_f.astype(BF16), w_proj_r.astype(BF16), w_out.astype(BF16))


def _mixer_path(x, mod, s0, want_state, grid_shift, p, tiles):
    b, t, d = x.shape
    u = _inproj(x, mod, p["norm_g"], p["w_in"], tiles["inproj"], p["w_in_segments"])
    fo = _fourier(u, t, p["fourier_width"], tiles["fourier_n1"], tiles["fourier"])
    r, k, v, kk, lw_f, lw_b, a_f, a_b = _rwkv_prep(
        u, p["mu_shift"], p["k_k"], p["w_up"], p["a_up"], p["w0"], p["a0"], grid_shift, tiles["prep"])
    res = _rwkv_scan(r, k, v, kk, lw_f, lw_b, a_f, a_b, p["k_a"], s0, want_state)
    y = _post(res[0], res[1], r, k, v, u, fo, x, mod, p["lnx_g"], p["lnx_b"], p["r_k"], p["final_g"],
              p["w_proj_f"], p["w_proj_r"], p["w_out"], tiles["post"])
    return y, (res[2] if want_state else None)


def _tiles(t, grid_shift):
    if not grid_shift:
        return dict(inproj=t, fourier=t, fourier_n1=1, prep=t, post=t)
    return dict(inproj=256, fourier=256, fourier_n1=4, prep=256, post=256)


def kernel(x_prompt, x_sample, state_rwkv, c, c_ctx, norm_g, w_ada, b_ada, w_in, mu_shift, w0, w_up, a0,
           a_up, k_k, k_a, r_k, lnx_g, lnx_b, w_proj_f, w_proj_r, w_out, final_g):
    depth = w_in.shape[0]
    assert depth == 1, "the final norm is fused into the single layer's post kernel"
    bp, tp, d = x_prompt.shape
    bs, ts, _ = x_sample.shape
    width = k_k.shape[1]
    fw = w_proj_f.shape[1]
    l = 0
    sh_end = 2 * fw + 3 * width
    rank2 = mu_shift.shape[1] - 3 * width
    n_in = w_in.shape[2]
    segments = ((0, 0, sh_end), (sh_end + rank2, sh_end, n_in - sh_end - rank2), (sh_end, n_in - rank2, rank2))
    p = dict(norm_g=norm_g[l], w_in=w_in[l].astype(BF16), w_in_segments=segments, fourier_width=fw, mu_shift=mu_shift[l], k_k=k_k[l], w_up=w_up[l],
             a_up=a_up[l], w0=w0[l], a0=a0[l], k_a=k_a[l], lnx_g=lnx_g[l], lnx_b=lnx_b[l], r_k=r_k[l],
             final_g=final_g, w_proj_f=w_proj_f[l], w_proj_r=w_proj_r[l], w_out=w_out[l])

    rows = 8
    cvec = jnp.concatenate([c_ctx[None], c, jnp.zeros((rows - 1 - bs, d), F32)], axis=0)
    m = _modulation(cvec, w_ada[l], b_ada[l])
    m3 = m.reshape(rows, 3, d)
    mod = jnp.concatenate([m3, jnp.zeros((rows, 5, d), F32)], axis=1)

    yp, sp = _mixer_path(x_prompt, jnp.broadcast_to(mod[0:1], (bp, 8, d)), None, True, False, p,
                         _tiles(tp, False))
    s0 = _pair_states(state_rwkv[:, l])
    ys, _ = _mixer_path(x_sample, mod[1:1 + bs], s0, False, True, p, _tiles(ts, True))
    new_state = _unpair_states(sp)[:, None]
    return yp, ys, new_state
```

```python
import functools
import math

import numpy as np
import jax
import jax.numpy as jnp
from jax import lax
from jax.experimental import pallas as pl
from jax.experimental.pallas import tpu as pltpu

F32 = jnp.float32
BF16 = jnp.bfloat16

LANES = 128
MXU_DIM = 256
VMEM_LIMIT = 56 * 1024 * 1024

HEAD = 64
GRID_W = 64
FOURIER_GROUP = 128
RMS_EPS = 1e-6
GN_EPS = 64e-5

TBLK = 128
CHUNK = 32
SCAN_PAIRS = 8


def _cparams(sem):
    return pltpu.CompilerParams(dimension_semantics=sem, vmem_limit_bytes=VMEM_LIMIT)


def _bdot(a, b):
    return jnp.dot(a.astype(BF16), b.astype(BF16), preferred_element_type=F32)


def _bdot_nt(a, b):
    return lax.dot_general(a.astype(BF16), b.astype(BF16), (((1,), (1,)), ((), ())),
                           preferred_element_type=F32)


def _bdot_tn(a, b):
    return lax.dot_general(a.astype(BF16), b.astype(BF16), (((0,), (0,)), ((), ())),
                           preferred_element_type=F32)


def _split2(x):
    hi = x.astype(BF16)
    lo = (x - hi.astype(F32)).astype(BF16)
    return hi, lo


def _dot_exact_rhs(x, m):
    hi, lo = _split2(x)
    return (jnp.dot(hi, m, preferred_element_type=F32) + jnp.dot(lo, m, preferred_element_type=F32))


def _silu(x):
    return x * jax.nn.sigmoid(x)


def _mod_kernel(c_ref, w_ref, b_ref, o_ref):
    c = c_ref[...]
    o_ref[...] = jnp.dot(_silu(c), w_ref[...], precision=lax.Precision.HIGHEST,
                         preferred_element_type=F32) + b_ref[...]


def _modulation(cvec, w_ada, b_ada):
    rows, d = cvec.shape
    n = w_ada.shape[1]
    tn = 512
    return pl.pallas_call(
        _mod_kernel,
        out_shape=jax.ShapeDtypeStruct((rows, n), F32),
        grid=(n // tn,),
        in_specs=[pl.BlockSpec((rows, d), lambda j: (0, 0)),
                  pl.BlockSpec((d, tn), lambda j: (0, j)),
                  pl.BlockSpec((1, tn), lambda j: (0, j))],
        out_specs=pl.BlockSpec((rows, tn), lambda j: (0, j)),
        compiler_params=_cparams(("arbitrary",)),
        name="adaln_modulation",
    )(cvec, w_ada, b_ada.reshape(1, n))


def _inproj_kernel(x_ref, mod_ref, g_ref, w_ref, o_ref, *, tn, segments):
    x = x_ref[0]
    y = x * lax.rsqrt(jnp.mean(x * x, axis=-1, keepdims=True) + RMS_EPS) * g_ref[...]
    m = mod_ref[0]
    h = (y * (1.0 + m[1:2]) + m[0:1]).astype(BF16)
    for src, dst, size in segments:
        for j0 in range(0, size, tn):
            w = min(tn, size - j0)
            o_ref[0, :, dst + j0:dst + j0 + w] = jnp.dot(h, w_ref[:, src + j0:src + j0 + w],
                                                         preferred_element_type=F32)


def _inproj(x, mod, norm_g, w_bf16, tm, segments):
    b, t, d = x.shape
    n = w_bf16.shape[1]
    return pl.pallas_call(
        functools.partial(_inproj_kernel, tn=2 * MXU_DIM, segments=segments),
        out_shape=jax.ShapeDtypeStruct((b, t, n), F32),
        grid=(b, t // tm),
        in_specs=[pl.BlockSpec((1, tm, d), lambda bi, i: (bi, i, 0)),
                  pl.BlockSpec((1, 8, d), lambda bi, i: (bi, 0, 0)),
                  pl.BlockSpec((1, d), lambda bi, i: (0, 0)),
                  pl.BlockSpec((d, n), lambda bi, i: (0, 0), pipeline_mode=pl.Buffered(1))],
        out_specs=pl.BlockSpec((1, tm, n), lambda bi, i: (bi, i, 0)),
        compiler_params=_cparams(("arbitrary", "arbitrary")),
        name="norm_mod_inproj",
    )(x, mod, norm_g.reshape(1, d), w_bf16)


_ROT = ((1.0, 0, -1.0, 1), (-1.0, 1, -1.0, 0), (-1.0, 0, 1.0, 1), (1.0, 1, 1.0, 0))


def _fourier_kernel(xf_ref, cc_ref, er_ref, ei_ref, o_ref, g_ref, il_ref, *, n1, t2, rows):
    width = xf_ref.shape[2]

    @pl.when(pl.program_id(1) == 0)
    def _():
        def body(i, carry):
            r0 = pl.multiple_of(i * rows, rows)
            pq = []
            for a in range(n1):
                x = xf_ref[0, pl.ds(a * t2 + r0, rows), :]
                parts = [_bdot(x[:, g * FOURIER_GROUP:(g + 1) * FOURIER_GROUP], cc_ref[...])
                         for g in range(width // FOURIER_GROUP)]
                pq.append((jnp.concatenate([p[:, :FOURIER_GROUP] for p in parts], axis=1),
                           jnp.concatenate([p[:, FOURIER_GROUP:] for p in parts], axis=1)))
            for f1 in range(n1):
                re = im = None
                for a in range(n1):
                    sr, cr, si, ci = _ROT[(a * f1) % 4]
                    tr, ti = sr * pq[a][cr], si * pq[a][ci]
                    re = tr if re is None else re + tr
                    im = ti if im is None else im + ti
                g_ref[f1, pl.ds(r0, rows), :] = re.astype(BF16)
                g_ref[f1, pl.ds(t2 + r0, rows), :] = im.astype(BF16)
            return carry
        lax.fori_loop(0, t2 // rows, body, 0)

    tf = er_ref.shape[1]
    for f1 in range(n1):
        res = (jnp.dot(er_ref[f1], g_ref[f1, 0:t2], preferred_element_type=F32)
               + jnp.dot(ei_ref[f1], g_ref[f1, t2:2 * t2], preferred_element_type=F32))
        if n1 == 1:
            o_ref[0] = res
        else:
            for g in range(width // LANES):
                il_ref[g, pl.ds(f1, tf, stride=n1), :] = res[:, g * LANES:(g + 1) * LANES]
    if n1 > 1:
        for g in range(width // LANES):
            o_ref[0, :, g * LANES:(g + 1) * LANES] = il_ref[g]


def _dft_consts(t, n1):
    t2 = t // n1
    g = FOURIER_GROUP
    j = np.arange(g)
    ang = 2.0 * np.pi * ((j[:, None] * j[None, :]) % g) / g
    cc = jnp.asarray(np.concatenate([np.cos(ang), np.sin(ang)], axis=1) / math.sqrt(g), dtype=F32).astype(BF16)
    scale = 1.0 / math.sqrt(t)
    if n1 == 1:
        k = np.arange(t)
        ang = 2.0 * np.pi * ((k[:, None] * k[None, :]) % t) / t
        tab = lambda m: jnp.asarray(m[None] * scale, dtype=F32).astype(BF16)
        return tab(np.cos(ang)), tab(np.sin(ang)), cc
    sa = 1 << (int(math.log2(t2)) // 2)
    sb = t2 // sa
    assert sa * sb == t2 and n1 * t2 == t
    tt = np.arange(t2)
    f1 = np.arange(n1)[:, None, None]
    bb = np.arange(sb)[None, :, None]
    ang_x = 2.0 * np.pi * ((tt[None, None, :] * (f1 + n1 * bb)) % t) / t
    aa = np.arange(sa)[:, None]
    ang_y = 2.0 * np.pi * ((tt[None, :] * aa) % sa) / sa
    xr, xi = jnp.asarray(np.cos(ang_x) * scale, F32), jnp.asarray(-np.sin(ang_x) * scale, F32)
    yr, yi = jnp.asarray(np.cos(ang_y), F32), jnp.asarray(-np.sin(ang_y), F32)
    xr, xi = xr[:, None], xi[:, None]
    yr, yi = yr[None, :, None], yi[None, :, None]
    re = (xr * yr - xi * yi).reshape(n1, t2, t2)
    im = (xr * yi + xi * yr).reshape(n1, t2, t2)
    return re.astype(BF16), (-im).astype(BF16), cc


def _fourier(u, t, width, n1, tf):
    b = u.shape[0]
    t2 = t // n1
    e_re, e_nim, cc = _dft_consts(t, n1)
    e_spec = pl.BlockSpec((n1, tf, t2), lambda bi, i: (0, i, 0))
    kern = functools.partial(_fourier_kernel, n1=n1, t2=t2, rows=min(t2, 256))
    return pl.pallas_call(
        kern,
        out_shape=jax.ShapeDtypeStruct((b, t, width), F32),
        grid=(b, t2 // tf),
        in_specs=[pl.BlockSpec((1, t, width), lambda bi, i: (bi, 0, 0)),
                  pl.BlockSpec((FOURIER_GROUP, 2 * FOURIER_GROUP), lambda bi, i: (0, 0)),
                  e_spec, e_spec],
        out_specs=pl.BlockSpec((1, n1 * tf, width), lambda bi, i: (bi, i, 0)),
        scratch_shapes=[pltpu.VMEM((n1, 2 * t2, width), BF16),
                        pltpu.VMEM((width // LANES, n1 * tf if n1 > 1 else 8, LANES), F32)],
        compiler_params=_cparams(("arbitrary", "arbitrary")),
        name="fourier_mix",
    )(u, cc, e_re, e_nim)


def _shift(x, prev_halo, next_halo, t_total, row0, grid_shift):
    rows, lanes = x.shape
    row = lax.broadcasted_iota(jnp.int32, (rows, lanes), 0)
    lane = lax.broadcasted_iota(jnp.int32, (rows, lanes), 1)
    before = pltpu.roll(x, 1, 0)
    after = pltpu.roll(x, rows - 1, 0)
    if not grid_shift:
        prev = jnp.where(row == 0, 0.0, before)
        nxt = jnp.where(row == rows - 1, 0.0, after)
        return jnp.where((lane & 1) == 0, prev, nxt)
    col = row & (GRID_W - 1)
    grow = row + row0
    left = jnp.where(col == 0, 0.0, before)
    right = jnp.where(col == GRID_W - 1, 0.0, after)
    up = jnp.concatenate([prev_halo, x[:rows - GRID_W]], axis=0)
    up = jnp.where(grow < GRID_W, 0.0, up)
    down = jnp.concatenate([x[GRID_W:], next_halo], axis=0)
    down = jnp.where(grow >= t_total - GRID_W, 0.0, down)
    m = lane & 3
    return jnp.where(m == 0, left, jnp.where(m == 1, right, jnp.where(m == 2, up, down)))


def _prep_kernel(*refs, grid_shift, t_total, width, cw):
    if grid_shift:
        (r_ref, k_ref, v_ref, d_ref, rp_ref, kp_ref, vp_ref, dp_ref, rn_ref, kn_ref, vn_ref, dn_ref,
         mu_ref, mud_ref, kk_ref, wcat_ref, bias_ref, seg_ref,
         ro_ref, ko_ref, vo_ref, kko_ref, lwf_ref, lwb_ref, af_ref, ab_ref) = refs
    else:
        (r_ref, k_ref, v_ref, d_ref, mu_ref, mud_ref, kk_ref, wcat_ref, bias_ref, seg_ref,
         ro_ref, ko_ref, vo_ref, kko_ref, lwf_ref, lwb_ref, af_ref, ab_ref) = refs
        rp_ref = kp_ref = vp_ref = dp_ref = rn_ref = kn_ref = vn_ref = dn_ref = None
    rows = r_ref.shape[1]
    row0 = pl.program_id(1) * rows

    def lerp(x_ref, p_ref, n_ref, mu, sl):
        x = x_ref[0, :, sl]
        ph = p_ref[0, :, sl] if grid_shift else None
        nh = n_ref[0, :, sl] if grid_shift else None
        s = _shift(x, ph, nh, t_total, row0, grid_shift)
        return x + mu * (s - x)

    for j in range(width // cw):
        sl = slice(j * cw, (j + 1) * cw)
        ro_ref[0, :, sl] = lerp(r_ref, rp_ref, rn_ref, mu_ref[0:1, sl], sl)
        vo_ref[0, :, sl] = lerp(v_ref, vp_ref, vn_ref, mu_ref[2:3, sl], sl)
        k = lerp(k_ref, kp_ref, kn_ref, mu_ref[1:2, sl], sl)
        ko_ref[0, :, sl] = k
        kk = k * kk_ref[0:1, sl]
        ss = _dot_exact_rhs(kk * kk, seg_ref[...])
        kko_ref[0, :, sl] = kk / jnp.maximum(jnp.sqrt(ss), 1e-12)

    dsl = slice(0, LANES)
    d = lerp(d_ref, dp_ref, dn_ref, mud_ref[...], dsl)
    lane = lax.broadcasted_iota(jnp.int32, d.shape, 1)
    d = jnp.where(lane < HEAD, jnp.tanh(d), d).astype(BF16)
    outs = (lwf_ref, lwb_ref, af_ref, ab_ref)
    for z in range(4):
        for j in range(width // cw):
            sl = slice(j * cw, (j + 1) * cw)
            wsl = slice(z * width + j * cw, z * width + (j + 1) * cw)
            pre = jnp.dot(d, wcat_ref[:, wsl], preferred_element_type=F32) + bias_ref[z:z + 1, sl]
            sg = jax.nn.sigmoid(pre)
            outs[z][0, :, sl] = sg * (-math.exp(-0.5)) if z < 2 else sg


def _rwkv_prep(u, mu_shift, k_k, w_up, a_up, w0, a0, grid_shift, tt):
    b, t, _ = u.shape
    width = k_k.shape[0]
    cw = MXU_DIM
    mu = mu_shift[:3 * width].reshape(3, width)
    mud = mu_shift[3 * width:].reshape(1, LANES)
    zpad = jnp.zeros((HEAD, width), F32)
    wcat = jnp.concatenate([jnp.concatenate([w_up[0], zpad], 0), jnp.concatenate([w_up[1], zpad], 0),
                            jnp.concatenate([zpad, a_up[0]], 0), jnp.concatenate([zpad, a_up[1]], 0)],
                           axis=1).astype(BF16)
    bias = jnp.concatenate([w0, a0], axis=0)
    seg = _seg_matrix(cw)
    d_blk = (u.shape[2] - LANES) // LANES
    main = [pl.BlockSpec((1, tt, width), lambda bi, i, c=c: (bi, i, c)) for c in (1, 2, 3)]
    main.append(pl.BlockSpec((1, tt, LANES), lambda bi, i: (bi, i, d_blk)))
    ins = [u, u, u, u]
    specs = list(main)
    if grid_shift:
        hb = tt // GRID_W
        last = t // GRID_W - 1
        for off in (-1, hb):
            def imap(bi, i, c, off=off):
                return (bi, jnp.clip(i * hb + off, 0, last), c)
            specs += [pl.BlockSpec((1, GRID_W, width), functools.partial(imap, c=c)) for c in (1, 2, 3)]
            specs.append(pl.BlockSpec((1, GRID_W, LANES), functools.partial(imap, c=d_blk)))
            ins += [u, u, u, u]
    const = lambda shape: pl.BlockSpec(shape, lambda bi, i: (0, 0))
    specs += [const((3, width)), const((1, LANES)), const((1, width)), const((LANES, 4 * width)),
              const((4, width)), const((cw, cw))]
    ins += [mu, mud, k_k.reshape(1, width), wcat, bias, seg]
    out_spec = pl.BlockSpec((1, tt, width), lambda bi, i: (bi, i, 0))
    kern = functools.partial(_prep_kernel, grid_shift=grid_shift, t_total=t, width=width, cw=cw)
    return pl.pallas_call(
        kern,
        out_shape=[jax.ShapeDtypeStruct((b, t, width), F32)] * 8,
        grid=(b, t // tt),
        in_specs=specs,
        out_specs=[out_spec] * 8,
        compiler_params=_cparams(("arbitrary", "arbitrary")),
        name="rwkv_prep",
    )(*ins)


def _seg_matrix(n):
    i = np.arange(n) // HEAD
    return jnp.asarray((i[:, None] == i[None, :]).astype(np.float32), dtype=BF16)


def _scan_consts():
    t = np.arange(TBLK)
    same = (t[:, None] // CHUNK) == (t[None, :] // CHUNK)
    tri_f = same & (t[None, :] <= t[:, None])
    tri_b = same & (t[None, :] >= t[:, None])
    mk = lambda m: jnp.asarray(m.astype(np.float32), dtype=BF16)
    return mk(tri_f), mk(tri_b)


def _heads_stacked(x, m0):
    return jnp.concatenate([jnp.where(m0, x, 0.0), jnp.where(m0, 0.0, x)], axis=0)


def _scan_chain(in_refs, ls, k_a, cum, reverse, st_ref, st_idx, y_ref):
    r_ref, k_ref, v_ref, kk_ref, lw_ref, a_ref = in_refs
    n = TBLK
    lw = lw_ref[0, :, ls]
    hi, lo = _split2(lw)
    ct = jnp.dot(cum, jnp.concatenate([hi, lo], axis=1), preferred_element_type=F32)
    yield
    cl = ct[:, :LANES] + ct[:, LANES:]
    n_chunks = n // CHUNK
    ends = [c * CHUNK if reverse else (c + 1) * CHUNK - 1 for c in range(n_chunks)]
    tot_rows = [cl[e:e + 1] for e in ends]
    tot = jnp.concatenate([jnp.broadcast_to(tr, (CHUNK, LANES)) for tr in tot_rows], axis=0)
    r, k, v, kk, a = r_ref[0, :, ls], k_ref[0, :, ls], v_ref[0, :, ls], kk_ref[0, :, ls], a_ref[0, :, ls]
    kd = k * (1.0 + (a - 1.0) * k_a)
    b = kk * a
    at = -kk * jnp.exp(cl - lw)
    rt = r * jnp.exp(cl)
    e_neg = jnp.exp(-cl)
    e_tail = jnp.exp(tot - cl)
    bh = b * e_tail
    kh = kd * e_tail
    w_chunk = [jnp.exp(tr) for tr in tot_rows]

    m0 = lax.broadcasted_iota(jnp.int32, (n, LANES), 1) < HEAD
    lhs = jnp.concatenate([_heads_stacked(at, m0), _heads_stacked(rt, m0)], axis=0)
    rhs = jnp.concatenate([b * e_neg, kd * e_neg], axis=0)
    g = _bdot_nt(lhs, rhs)
    yield

    row = lax.broadcasted_iota(jnp.int32, (n, 2 * n), 0)
    col = lax.broadcasted_iota(jnp.int32, (n, 2 * n), 1) & (n - 1)
    same = (row // CHUNK) == (col // CHUNK)
    if reverse:
        strict, incl = same & (col > row), same & (col >= row)
    else:
        strict, incl = same & (col < row), same & (col <= row)
    pick = lambda r0, cs: jnp.concatenate([g[r0:r0 + n, cs], g[r0 + n:r0 + 2 * n, cs]], axis=1)
    left, right = slice(0, n), slice(n, 2 * n)
    a_ab = jnp.where(strict, pick(0, left), 0.0)
    a_kr = jnp.concatenate([jnp.where(strict, pick(0, right), 0.0),
                            jnp.where(incl, pick(2 * n, right), 0.0)], axis=0).astype(BF16)
    a_rb = jnp.where(incl, pick(2 * n, left), 0.0).astype(BF16)

    blocks = 2 * n_chunks
    brow = lax.broadcasted_iota(jnp.int32, (2 * n, 2 * n), 0)
    bcol = lax.broadcasted_iota(jnp.int32, (2 * n, 2 * n), 1)
    on_block = (brow // CHUNK) == (bcol // CHUNK)
    spread = lambda x: jnp.where(on_block, jnp.concatenate([x.astype(BF16)] * blocks, axis=0), 0.0)
    steps = int(math.log2(CHUNK)) - 1
    q = a_ab[:CHUNK]
    for c in range(1, n_chunks):
        q = q + a_ab[c * CHUNK:(c + 1) * CHUNK]
    p = _bdot(q, spread(q))
    yield
    for _ in range(steps - 1):
        res = _bdot(jnp.concatenate([p, q], axis=0), spread(p))
        yield
        q = q + p + res[CHUNK:]
        p = res[:CHUNK]
    fin = _bdot(q, spread(p))
    yield
    q = q + p + fin
    q16 = jnp.where(same, jnp.concatenate([q] * n_chunks, axis=0), 0.0).astype(BF16)

    v_s = _heads_stacked(v, m0).astype(BF16)
    qa = jnp.dot(q16, _heads_stacked(at, m0).astype(BF16), preferred_element_type=F32)
    kr = jnp.dot(a_kr, v_s, preferred_element_type=F32)
    akv, rkv = kr[:n], kr[n:]
    yield
    a_hat = at + qa
    uvq = jnp.dot(q16, _heads_stacked(akv, m0).astype(BF16), preferred_element_type=F32)
    yield
    uv = akv + uvq
    both = jnp.dot(a_rb, jnp.concatenate([_heads_stacked(a_hat, m0), _heads_stacked(uv, m0)],
                                         axis=1).astype(BF16), preferred_element_type=F32)
    yield
    r_hat = rt + both[:, :LANES]
    yv = both[:, LANES:] + rkv

    srow = lax.broadcasted_iota(jnp.int32, (LANES, LANES), 0)
    scol = lax.broadcasted_iota(jnp.int32, (LANES, LANES), 1)
    pair_diag = (srow < HEAD) == (scol < HEAD)
    state = st_ref[st_idx]
    for c in (range(n_chunks - 1, -1, -1) if reverse else range(n_chunks)):
        sl = slice(c * CHUNK, (c + 1) * CHUNK)
        proj = _bdot_nt(jnp.concatenate([a_hat[sl], r_hat[sl]], axis=0), state)
        yield
        u_c = proj[:CHUNK] + uv[sl]
        y_ref[0, sl, ls] = proj[CHUNK:] + yv[sl]
        upd = _bdot_tn(jnp.concatenate([u_c, v[sl]], axis=0), jnp.concatenate([bh[sl], kh[sl]], axis=0))
        yield
        state = state * w_chunk[c] + jnp.where(pair_diag, upd, 0.0)
    st_ref[st_idx] = state


def _run_interleaved(chains):
    chains = list(chains)
    while chains:
        alive = []
        for ch in chains:
            try:
                next(ch)
                alive.append(ch)
            except StopIteration:
                pass
        chains = alive


def _scan_kernel(*refs, has_init, want_state, groups):
    fwd_refs, bwd_refs = refs[:6], refs[6:12]
    ka_ref, cumf_ref, cumb_ref = refs[12:15]
    pos = 15
    s0_ref = None
    if has_init:
        s0_ref = refs[pos]
        pos += 1
    yf_ref, yb_ref = refs[pos:pos + 2]
    pos += 2
    so_ref = None
    if want_state:
        so_ref = refs[pos]
        pos += 1
    st_ref = refs[pos]
    i = pl.program_id(2)

    @pl.when(i == 0)
    def _():
        if has_init:
            st_ref[...] = s0_ref[0]
        else:
            st_ref[...] = jnp.zeros(st_ref.shape, F32)

    chains = []
    for g in range(groups):
        ls = slice(g * LANES, (g + 1) * LANES)
        k_a = ka_ref[:, ls]
        chains.append(_scan_chain(fwd_refs, ls, k_a, cumf_ref[...], False, st_ref, (0, g), yf_ref))
        chains.append(_scan_chain(bwd_refs, ls, k_a, cumb_ref[...], True, st_ref, (1, g), yb_ref))
    _run_interleaved(chains)

    if want_state:
        @pl.when(i == pl.num_programs(2) - 1)
        def _():
            for z in range(2):
                for g in range(groups):
                    s = st_ref[z, g]
                    so_ref[0, z, 2 * g] = s[:HEAD, :HEAD]
                    so_ref[0, z, 2 * g + 1] = s[HEAD:, HEAD:]


def _rwkv_scan(r, k, v, kk, lw_f, lw_b, a_f, a_b, k_a, s0, want_state):
    b, t, width = r.shape
    pairs = width // LANES
    g = SCAN_PAIRS
    gw = g * LANES
    nt = t // TBLK
    fwd = pl.BlockSpec((1, TBLK, gw), lambda bi, p, i: (bi, i, p))
    bwd = pl.BlockSpec((1, TBLK, gw), lambda bi, p, i: (bi, nt - 1 - i, p))
    cmat = pl.BlockSpec((TBLK, TBLK), lambda bi, p, i: (0, 0))
    st_spec = pl.BlockSpec((1, 2, g, LANES, LANES), lambda bi, p, i: (bi, 0, p, 0, 0))
    specs = [fwd] * 6 + [bwd] * 6 + [pl.BlockSpec((1, gw), lambda bi, p, i: (0, p)), cmat, cmat]
    ins = [r, k, v, kk, lw_f, a_f, r, k, v, kk, lw_b, a_b, k_a.reshape(1, width), *_scan_consts()]
    if s0 is not None:
        specs.append(st_spec)
        ins.append(s0)
    y_shape = jax.ShapeDtypeStruct((b, t, width), F32)
    out_shape = [y_shape, y_shape]
    out_specs = [fwd, bwd]
    if want_state:
        out_shape.append(jax.ShapeDtypeStruct((b, 2, 2 * pairs, HEAD, HEAD), F32))
        out_specs.append(pl.BlockSpec((1, 2, 2 * g, HEAD, HEAD), lambda bi, p, i: (bi, 0, p, 0, 0)))
    kern = functools.partial(_scan_kernel, has_init=s0 is not None, want_state=want_state, groups=g)
    return pl.pallas_call(
        kern,
        out_shape=out_shape,
        grid=(b, pairs // g, nt),
        in_specs=specs,
        out_specs=out_specs,
        scratch_shapes=[pltpu.VMEM((2, g, LANES, LANES), F32)],
        compiler_params=_cparams(("arbitrary", "arbitrary", "arbitrary")),
        name="rwkv7_scan",
    )(*ins)


def _pair_states(s):
    b, z, h, n, _ = s.shape
    sp = s.reshape(b, z, h // 2, 2, n, n)
    zero = jnp.zeros_like(sp[:, :, :, 0])
    top = jnp.concatenate([sp[:, :, :, 0], zero], axis=-1)
    bot = jnp.concatenate([zero, sp[:, :, :, 1]], axis=-1)
    return jnp.concatenate([top, bot], axis=-2)


def _post_kernel(yf_ref, yb_ref, r_ref, k_ref, v_ref, gf_ref, gr_ref, g0_ref, g1_ref, fo_ref, x_ref, mod_ref,
                 vec_ref, seg_ref, wf_ref, wr_ref, wo_ref, o_ref, z_ref, *, cw):
    width = yf_ref.shape[2]
    inv_n = 1.0 / HEAD
    for j in range(width // cw):
        sl = slice(j * cw, (j + 1) * cw)
        seg = seg_ref[...]
        y = yf_ref[0, :, sl] + yb_ref[0, :, sl]
        mean = _dot_exact_rhs(y, seg) * inv_n
        d = y - mean
        var = _dot_exact_rhs(d * d, seg) * inv_n
        yn = d * lax.rsqrt(var + GN_EPS) * vec_ref[0:1, sl] + vec_ref[1:2, sl]
        rk = _dot_exact_rhs(r_ref[0, :, sl] * k_ref[0, :, sl] * vec_ref[2:3, sl], seg)
        z = (yn + rk * v_ref[0, :, sl]) * _silu(gr_ref[0, :, sl])
        z_ref[:, sl] = z.astype(BF16)
    out_r = jnp.dot(z_ref[...], wr_ref[...], preferred_element_type=F32)
    zf = (fo_ref[0] * _silu(gf_ref[0])).astype(BF16)
    out_f = jnp.dot(zf, wf_ref[...], preferred_element_type=F32)
    merged = jax.nn.sigmoid(g0_ref[0]) * out_f + jax.nn.sigmoid(g1_ref[0]) * out_r
    o = jnp.dot(merged.astype(BF16), wo_ref[...], preferred_element_type=F32)
    xn = x_ref[0] + mod_ref[0][2:3] * o
    o_ref[0] = xn * lax.rsqrt(jnp.mean(xn * xn, axis=-1, keepdims=True) + RMS_EPS) * vec_ref[3:4, :]


def _post(yf, yb, r, k, v, u, fo, x, mod, lnx_g, lnx_b, r_k, final_g, w_proj_f, w_proj_r, w_out, tm):
    b, t, d = x.shape
    width = yf.shape[2]
    fw = fo.shape[2]
    cw = MXU_DIM
    vec = jnp.stack([lnx_g, lnx_b, r_k.reshape(-1), final_g], axis=0)
    tok = lambda w, c: pl.BlockSpec((1, tm, w), lambda bi, i: (bi, i, c))
    const = lambda shape: pl.BlockSpec(shape, lambda bi, i: (0,) * len(shape))
    gr_off = 2 * fw + 3 * width
    mg_off = gr_off + width
    assert gr_off % width == 0 and mg_off % d == 0
    gr_blk, mg_blk = gr_off // width, mg_off // d
    specs = [tok(width, 0)] * 5 + [tok(fw, 1), tok(width, gr_blk), tok(d, mg_blk), tok(d, mg_blk + 1),
                                   tok(fw, 0), tok(d, 0), pl.BlockSpec((1, 8, d), lambda bi, i: (bi, 0, 0)),
                                   const((4, d)), const((cw, cw)), const((fw, d)), const((width, d)),
                                   const((d, d))]
    return pl.pallas_call(
        functools.partial(_post_kernel, cw=cw),
        out_shape=jax.ShapeDtypeStruct((b, t, d), F32),
        grid=(b, t // tm),
        in_specs=specs,
        out_specs=tok(d, 0),
        scratch_shapes=[pltpu.VMEM((tm, width), BF16)],
        compiler_params=_cparams(("arbitrary", "arbitrary")),
        name="post_mix",
    )(yf, yb, r, k, v, u, u, u, u, fo, x, mod, vec, _seg_matrix(cw),
      w_proj_f.astype(BF16), w_proj_r.astype(BF16), w_out.astype(BF16))


def _mixer_path(x, mod, s0, want_state, grid_shift, p, tiles):
    b, t, d = x.shape
    u = _inproj(x, mod, p["norm_g"], p["w_in"], tiles["inproj"], p["w_in_segments"])
    fo = _fourier(u, t, p["fourier_width"], tiles["fourier_n1"], tiles["fourier"])
    r, k, v, kk, lw_f, lw_b, a_f, a_b = _rwkv_prep(
        u, p["mu_shift"], p["k_k"], p["w_up"], p["a_up"], p["w0"], p["a0"], grid_shift, tiles["prep"])
    res = _rwkv_scan(r, k, v, kk, lw_f, lw_b, a_f, a_b, p["k_a"], s0, want_state)
    y = _post(res[0], res[1], r, k, v, u, fo, x, mod, p["lnx_g"], p["lnx_b"], p["r_k"], p["final_g"],
              p["w_proj_f"], p["w_proj_r"], p["w_out"], tiles["post"])
    return y, (res[2] if want_state else None)


def _tiles(t, grid_shift):
    if not grid_shift:
        return dict(inproj=t, fourier=t, fourier_n1=1, prep=t, post=t)
    return dict(inproj=256, fourier=256, fourier_n1=4, prep=256, post=256)


def kernel(x_prompt, x_sample, state_rwkv, c, c_ctx, norm_g, w_ada, b_ada, w_in, mu_shift, w0, w_up, a0,
           a_up, k_k, k_a, r_k, lnx_g, lnx_b, w_proj_f, w_proj_r, w_out, final_g):
    depth = w_in.shape[0]
    assert depth == 1, "the final norm is fused into the single layer's post kernel"
    bp, tp, d = x_prompt.shape
    bs, ts, _ = x_sample.shape
    width = k_k.shape[1]
    fw = w_proj_f.shape[1]
    l = 0
    sh_end = 2 * fw + 3 * width
    rank2 = mu_shift.shape[1] - 3 * width
    n_in = w_in.shape[2]
    segments = ((0, 0, sh_end), (sh_end + rank2, sh_end, n_in - sh_end - rank2), (sh_end, n_in - rank2, rank2))
    p = dict(norm_g=norm_g[l], w_in=w_in[l].astype(BF16), w_in_segments=segments, fourier_width=fw, mu_shift=mu_shift[l], k_k=k_k[l], w_up=w_up[l],
             a_up=a_up[l], w0=w0[l], a0=a0[l], k_a=k_a[l], lnx_g=lnx_g[l], lnx_b=lnx_b[l], r_k=r_k[l],
             final_g=final_g, w_proj_f=w_proj_f[l], w_proj_r=w_proj_r[l], w_out=w_out[l])

    rows = 8
    cvec = jnp.concatenate([c_ctx[None], c, jnp.zeros((rows - 1 - bs, d), F32)], axis=0)
    m = _modulation(cvec, w_ada[l], b_ada[l])
    m3 = m.reshape(rows, 3, d)
    mod = jnp.concatenate([m3, jnp.zeros((rows, 5, d), F32)], axis=1)

    yp, sp = _mixer_path(x_prompt, jnp.broadcast_to(mod[0:1], (bp, 8, d)), None, True, False, p,
                         _tiles(tp, False))
    s0 = _pair_states(state_rwkv[:, l])
    ys, _ = _mixer_path(x_sample, mod[1:1 + bs], s0, False, True, p, _tiles(ts, True))
    new_state = sp[:, None]
    return yp, ys, new_state
```

```python
import functools
import math

import numpy as np
import jax
import jax.numpy as jnp
from jax import lax
from jax.experimental import pallas as pl
from jax.experimental.pallas import tpu as pltpu

F32 = jnp.float32
BF16 = jnp.bfloat16

LANES = 128
MXU_DIM = 256
VMEM_LIMIT = 56 * 1024 * 1024

HEAD = 64
GRID_W = 64
FOURIER_GROUP = 128
RMS_EPS = 1e-6
GN_EPS = 64e-5

TBLK = 128
CHUNK = 32
SCAN_PAIRS = 8


def _cparams(sem):
    return pltpu.CompilerParams(dimension_semantics=sem, vmem_limit_bytes=VMEM_LIMIT)


def _bdot(a, b):
    return jnp.dot(a.astype(BF16), b.astype(BF16), preferred_element_type=F32)


def _bdot_nt(a, b):
    return lax.dot_general(a.astype(BF16), b.astype(BF16), (((1,), (1,)), ((), ())),
                           preferred_element_type=F32)


def _bdot_tn(a, b):
    return lax.dot_general(a.astype(BF16), b.astype(BF16), (((0,), (0,)), ((), ())),
                           preferred_element_type=F32)


def _split2(x):
    hi = x.astype(BF16)
    lo = (x - hi.astype(F32)).astype(BF16)
    return hi, lo


def _dot_exact_rhs(x, m):
    hi, lo = _split2(x)
    return (jnp.dot(hi, m, preferred_element_type=F32) + jnp.dot(lo, m, preferred_element_type=F32))


def _silu(x):
    return x * jax.nn.sigmoid(x)


def _mod_kernel(c_ref, w_ref, b_ref, o_ref):
    c = c_ref[...]
    o_ref[...] = jnp.dot(_silu(c), w_ref[...], precision=lax.Precision.HIGHEST,
                         preferred_element_type=F32) + b_ref[...]


def _modulation(cvec, w_ada, b_ada):
    rows, d = cvec.shape
    n = w_ada.shape[1]
    tn = 512
    return pl.pallas_call(
        _mod_kernel,
        out_shape=jax.ShapeDtypeStruct((rows, n), F32),
        grid=(n // tn,),
        in_specs=[pl.BlockSpec((rows, d), lambda j: (0, 0)),
                  pl.BlockSpec((d, tn), lambda j: (0, j)),
                  pl.BlockSpec((1, tn), lambda j: (0, j))],
        out_specs=pl.BlockSpec((rows, tn), lambda j: (0, j)),
        compiler_params=_cparams(("arbitrary",)),
        name="adaln_modulation",
    )(cvec, w_ada, b_ada.reshape(1, n))


def _inproj_kernel(x_ref, mod_ref, g_ref, w_ref, o_ref, *, tn, segments):
    x = x_ref[0]
    y = x * lax.rsqrt(jnp.mean(x * x, axis=-1, keepdims=True) + RMS_EPS) * g_ref[...]
    m = mod_ref[0]
    h = (y * (1.0 + m[1:2]) + m[0:1]).astype(BF16)
    for src, dst, size in segments:
        for j0 in range(0, size, tn):
            w = min(tn, size - j0)
            o_ref[0, :, dst + j0:dst + j0 + w] = jnp.dot(h, w_ref[:, src + j0:src + j0 + w],
                                                         preferred_element_type=F32)


def _inproj(x, mod, norm_g, w_bf16, tm, segments):
    b, t, d = x.shape
    n = w_bf16.shape[1]
    return pl.pallas_call(
        functools.partial(_inproj_kernel, tn=2 * MXU_DIM, segments=segments),
        out_shape=jax.ShapeDtypeStruct((b, t, n), F32),
        grid=(b, t // tm),
        in_specs=[pl.BlockSpec((1, tm, d), lambda bi, i: (bi, i, 0)),
                  pl.BlockSpec((1, 8, d), lambda bi, i: (bi, 0, 0)),
                  pl.BlockSpec((1, d), lambda bi, i: (0, 0)),
                  pl.BlockSpec((d, n), lambda bi, i: (0, 0), pipeline_mode=pl.Buffered(1))],
        out_specs=pl.BlockSpec((1, tm, n), lambda bi, i: (bi, i, 0)),
        compiler_params=_cparams(("arbitrary", "arbitrary")),
        name="norm_mod_inproj",
    )(x, mod, norm_g.reshape(1, d), w_bf16)


_ROT = ((1.0, 0, -1.0, 1), (-1.0, 1, -1.0, 0), (-1.0, 0, 1.0, 1), (1.0, 1, 1.0, 0))


def _fourier_kernel(xf_ref, cc_ref, er_ref, ei_ref, o_ref, g_ref, il_ref, *, n1, t2, rows):
    width = xf_ref.shape[2]

    @pl.when(pl.program_id(1) == 0)
    def _():
        def body(i, carry):
            r0 = pl.multiple_of(i * rows, rows)
            pq = []
            for a in range(n1):
                x = xf_ref[0, pl.ds(a * t2 + r0, rows), :]
                parts = [_bdot(x[:, g * FOURIER_GROUP:(g + 1) * FOURIER_GROUP], cc_ref[...])
                         for g in range(width // FOURIER_GROUP)]
                pq.append((jnp.concatenate([p[:, :FOURIER_GROUP] for p in parts], axis=1),
                           jnp.concatenate([p[:, FOURIER_GROUP:] for p in parts], axis=1)))
            for f1 in range(n1):
                re = im = None
                for a in range(n1):
                    sr, cr, si, ci = _ROT[(a * f1) % 4]
                    tr, ti = sr * pq[a][cr], si * pq[a][ci]
                    re = tr if re is None else re + tr
                    im = ti if im is None else im + ti
                g_ref[f1, pl.ds(r0, rows), :] = re.astype(BF16)
                g_ref[f1, pl.ds(t2 + r0, rows), :] = im.astype(BF16)
            return carry
        lax.fori_loop(0, t2 // rows, body, 0)

    tf = er_ref.shape[1]
    for f1 in range(n1):
        res = (jnp.dot(er_ref[f1], g_ref[f1, 0:t2], preferred_element_type=F32)
               + jnp.dot(ei_ref[f1], g_ref[f1, t2:2 * t2], preferred_element_type=F32))
        if n1 == 1:
            o_ref[0] = res
        else:
            for g in range(width // LANES):
                il_ref[g, pl.ds(f1, tf, stride=n1), :] = res[:, g * LANES:(g + 1) * LANES]
    if n1 > 1:
        for g in range(width // LANES):
            o_ref[0, :, g * LANES:(g + 1) * LANES] = il_ref[g]


def _dft_consts(t, n1):
    t2 = t // n1
    g = FOURIER_GROUP
    j = np.arange(g)
    ang = 2.0 * np.pi * ((j[:, None] * j[None, :]) % g) / g
    cc = jnp.asarray(np.concatenate([np.cos(ang), np.sin(ang)], axis=1) / math.sqrt(g), dtype=F32).astype(BF16)
    scale = 1.0 / math.sqrt(t)
    if n1 == 1:
        k = np.arange(t)
        ang = 2.0 * np.pi * ((k[:, None] * k[None, :]) % t) / t
        tab = lambda m: jnp.asarray(m[None] * scale, dtype=F32).astype(BF16)
        return tab(np.cos(ang)), tab(np.sin(ang)), cc
    sa = 1 << (int(math.log2(t2)) // 2)
    sb = t2 // sa
    assert sa * sb == t2 and n1 * t2 == t
    tt = np.arange(t2)
    f1 = np.arange(n1)[:, None, None]
    bb = np.arange(sb)[None, :, None]
    ang_x = 2.0 * np.pi * ((tt[None, None, :] * (f1 + n1 * bb)) % t) / t
    aa = np.arange(sa)[:, None]
    ang_y = 2.0 * np.pi * ((tt[None, :] * aa) % sa) / sa
    xr, xi = jnp.asarray(np.cos(ang_x) * scale, F32), jnp.asarray(-np.sin(ang_x) * scale, F32)
    yr, yi = jnp.asarray(np.cos(ang_y), F32), jnp.asarray(-np.sin(ang_y), F32)
    xr, xi = xr[:, None], xi[:, None]
    yr, yi = yr[None, :, None], yi[None, :, None]
    re = (xr * yr - xi * yi).reshape(n1, t2, t2)
    im = (xr * yi + xi * yr).reshape(n1, t2, t2)
    return re.astype(BF16), (-im).astype(BF16), cc


def _fourier(u, t, width, n1, tf):
    b = u.shape[0]
    t2 = t // n1
    e_re, e_nim, cc = _dft_consts(t, n1)
    e_spec = pl.BlockSpec((n1, tf, t2), lambda bi, i: (0, i, 0))
    kern = functools.partial(_fourier_kernel, n1=n1, t2=t2, rows=min(t2, 256))
    return pl.pallas_call(
        kern,
        out_shape=jax.ShapeDtypeStruct((b, t, width), F32),
        grid=(b, t2 // tf),
        in_specs=[pl.BlockSpec((1, t, width), lambda bi, i: (bi, 0, 0)),
                  pl.BlockSpec((FOURIER_GROUP, 2 * FOURIER_GROUP), lambda bi, i: (0, 0)),
                  e_spec, e_spec],
        out_specs=pl.BlockSpec((1, n1 * tf, width), lambda bi, i: (bi, i, 0)),
        scratch_shapes=[pltpu.VMEM((n1, 2 * t2, width), BF16),
                        pltpu.VMEM((width // LANES, n1 * tf if n1 > 1 else 8, LANES), F32)],
        compiler_params=_cparams(("arbitrary", "arbitrary")),
        name="fourier_mix",
    )(u, cc, e_re, e_nim)


def _shift(x, prev_halo, next_halo, t_total, row0, grid_shift):
    rows, lanes = x.shape
    row = lax.broadcasted_iota(jnp.int32, (rows, lanes), 0)
    lane = lax.broadcasted_iota(jnp.int32, (rows, lanes), 1)
    before = pltpu.roll(x, 1, 0)
    after = pltpu.roll(x, rows - 1, 0)
    if not grid_shift:
        prev = jnp.where(row == 0, 0.0, before)
        nxt = jnp.where(row == rows - 1, 0.0, after)
        return jnp.where((lane & 1) == 0, prev, nxt)
    col = row & (GRID_W - 1)
    grow = row + row0
    left = jnp.where(col == 0, 0.0, before)
    right = jnp.where(col == GRID_W - 1, 0.0, after)
    up = jnp.concatenate([prev_halo, x[:rows - GRID_W]], axis=0)
    up = jnp.where(grow < GRID_W, 0.0, up)
    down = jnp.concatenate([x[GRID_W:], next_halo], axis=0)
    down = jnp.where(grow >= t_total - GRID_W, 0.0, down)
    m = lane & 3
    return jnp.where(m == 0, left, jnp.where(m == 1, right, jnp.where(m == 2, up, down)))


def _prep_kernel(*refs, grid_shift, t_total, width, cw):
    if grid_shift:
        (r_ref, k_ref, v_ref, d_ref, rp_ref, kp_ref, vp_ref, dp_ref, rn_ref, kn_ref, vn_ref, dn_ref,
         mu_ref, mud_ref, kk_ref, wcat_ref, bias_ref, seg_ref,
         ro_ref, ko_ref, vo_ref, kko_ref, lwf_ref, lwb_ref, af_ref, ab_ref) = refs
    else:
        (r_ref, k_ref, v_ref, d_ref, mu_ref, mud_ref, kk_ref, wcat_ref, bias_ref, seg_ref,
         ro_ref, ko_ref, vo_ref, kko_ref, lwf_ref, lwb_ref, af_ref, ab_ref) = refs
        rp_ref = kp_ref = vp_ref = dp_ref = rn_ref = kn_ref = vn_ref = dn_ref = None
    rows = r_ref.shape[1]
    row0 = pl.program_id(1) * rows

    def lerp(x_ref, p_ref, n_ref, mu, sl):
        x = x_ref[0, :, sl]
        ph = p_ref[0, :, sl] if grid_shift else None
        nh = n_ref[0, :, sl] if grid_shift else None
        s = _shift(x, ph, nh, t_total, row0, grid_shift)
        return x + mu * (s - x)

    for j in range(width // cw):
        sl = slice(j * cw, (j + 1) * cw)
        ro_ref[0, :, sl] = lerp(r_ref, rp_ref, rn_ref, mu_ref[0:1, sl], sl)
        vo_ref[0, :, sl] = lerp(v_ref, vp_ref, vn_ref, mu_ref[2:3, sl], sl)
        k = lerp(k_ref, kp_ref, kn_ref, mu_ref[1:2, sl], sl)
        ko_ref[0, :, sl] = k
        kk = k * kk_ref[0:1, sl]
        ss = _dot_exact_rhs(kk * kk, seg_ref[...])
        kko_ref[0, :, sl] = kk / jnp.maximum(jnp.sqrt(ss), 1e-12)

    dsl = slice(0, LANES)
    d = lerp(d_ref, dp_ref, dn_ref, mud_ref[...], dsl)
    lane = lax.broadcasted_iota(jnp.int32, d.shape, 1)
    d = jnp.where(lane < HEAD, jnp.tanh(d), d).astype(BF16)
    outs = (lwf_ref, lwb_ref, af_ref, ab_ref)
    for z in range(4):
        for j in range(width // cw):
            sl = slice(j * cw, (j + 1) * cw)
            wsl = slice(z * width + j * cw, z * width + (j + 1) * cw)
            pre = jnp.dot(d, wcat_ref[:, wsl], preferred_element_type=F32) + bias_ref[z:z + 1, sl]
            sg = jax.nn.sigmoid(pre)
            outs[z][0, :, sl] = sg * (-math.exp(-0.5)) if z < 2 else sg


def _rwkv_prep(u, mu_shift, k_k, w_up, a_up, w0, a0, grid_shift, tt):
    b, t, _ = u.shape
    width = k_k.shape[0]
    cw = MXU_DIM
    mu = mu_shift[:3 * width].reshape(3, width)
    mud = mu_shift[3 * width:].reshape(1, LANES)
    zpad = jnp.zeros((HEAD, width), F32)
    wcat = jnp.concatenate([jnp.concatenate([w_up[0], zpad], 0), jnp.concatenate([w_up[1], zpad], 0),
                            jnp.concatenate([zpad, a_up[0]], 0), jnp.concatenate([zpad, a_up[1]], 0)],
                           axis=1).astype(BF16)
    bias = jnp.concatenate([w0, a0], axis=0)
    seg = _seg_matrix(cw)
    d_blk = (u.shape[2] - LANES) // LANES
    main = [pl.BlockSpec((1, tt, width), lambda bi, i, c=c: (bi, i, c)) for c in (1, 2, 3)]
    main.append(pl.BlockSpec((1, tt, LANES), lambda bi, i: (bi, i, d_blk)))
    ins = [u, u, u, u]
    specs = list(main)
    if grid_shift:
        hb = tt // GRID_W
        last = t // GRID_W - 1
        for off in (-1, hb):
            def imap(bi, i, c, off=off):
                return (bi, jnp.clip(i * hb + off, 0, last), c)
            specs += [pl.BlockSpec((1, GRID_W, width), functools.partial(imap, c=c)) for c in (1, 2, 3)]
            specs.append(pl.BlockSpec((1, GRID_W, LANES), functools.partial(imap, c=d_blk)))
            ins += [u, u, u, u]
    const = lambda shape: pl.BlockSpec(shape, lambda bi, i: (0, 0))
    specs += [const((3, width)), const((1, LANES)), const((1, width)), const((LANES, 4 * width)),
              const((4, width)), const((cw, cw))]
    ins += [mu, mud, k_k.reshape(1, width), wcat, bias, seg]
    out_spec = pl.BlockSpec((1, tt, width), lambda bi, i: (bi, i, 0))
    kern = functools.partial(_prep_kernel, grid_shift=grid_shift, t_total=t, width=width, cw=cw)
    return pl.pallas_call(
        kern,
        out_shape=[jax.ShapeDtypeStruct((b, t, width), F32)] * 8,
        grid=(b, t // tt),
        in_specs=specs,
        out_specs=[out_spec] * 8,
        compiler_params=_cparams(("arbitrary", "arbitrary")),
        name="rwkv_prep",
    )(*ins)


def _seg_matrix(n):
    i = np.arange(n) // HEAD
    return jnp.asarray((i[:, None] == i[None, :]).astype(np.float32), dtype=BF16)


def _heads_stacked(x, m0):
    return jnp.concatenate([jnp.where(m0, x, 0.0), jnp.where(m0, 0.0, x)], axis=0)


def _scan_chain(in_refs, ls, k_a, reverse, st_ref, st_idx, y_ref):
    r_ref, k_ref, v_ref, kk_ref, lw_ref, a_ref = in_refs
    n = TBLK
    lw = lw_ref[0, :, ls]
    cl = lw
    pos = lax.broadcasted_iota(jnp.int32, (n, LANES), 0) & (CHUNK - 1)
    step = 1
    while step < CHUNK:
        if reverse:
            cl = cl + jnp.where(pos < CHUNK - step, pltpu.roll(cl, n - step, 0), 0.0)
        else:
            cl = cl + jnp.where(pos >= step, pltpu.roll(cl, step, 0), 0.0)
        step *= 2
    n_chunks = n // CHUNK
    ends = [c * CHUNK if reverse else (c + 1) * CHUNK - 1 for c in range(n_chunks)]
    tot_rows = [cl[e:e + 1] for e in ends]
    tot = jnp.concatenate([jnp.broadcast_to(tr, (CHUNK, LANES)) for tr in tot_rows], axis=0)
    r, k, v, kk, a = r_ref[0, :, ls], k_ref[0, :, ls], v_ref[0, :, ls], kk_ref[0, :, ls], a_ref[0, :, ls]
    kd = k * (1.0 + (a - 1.0) * k_a)
    b = kk * a
    at = -kk * jnp.exp(cl - lw)
    rt = r * jnp.exp(cl)
    e_neg = jnp.exp(-cl)
    e_tail = jnp.exp(tot - cl)
    bh = b * e_tail
    kh = kd * e_tail
    w_chunk = [jnp.exp(tr) for tr in tot_rows]

    m0 = lax.broadcasted_iota(jnp.int32, (n, LANES), 1) < HEAD
    lhs = jnp.concatenate([_heads_stacked(at, m0), _heads_stacked(rt, m0)], axis=0)
    rhs = jnp.concatenate([b * e_neg, kd * e_neg], axis=0)
    g = _bdot_nt(lhs, rhs)
    yield

    row = lax.broadcasted_iota(jnp.int32, (n, 2 * n), 0)
    col = lax.broadcasted_iota(jnp.int32, (n, 2 * n), 1) & (n - 1)
    same = (row // CHUNK) == (col // CHUNK)
    if reverse:
        strict, incl = same & (col > row), same & (col >= row)
    else:
        strict, incl = same & (col < row), same & (col <= row)
    pick = lambda r0, cs: jnp.concatenate([g[r0:r0 + n, cs], g[r0 + n:r0 + 2 * n, cs]], axis=1)
    left, right = slice(0, n), slice(n, 2 * n)
    a_ab = jnp.where(strict, pick(0, left), 0.0)
    a_kr = jnp.concatenate([jnp.where(strict, pick(0, right), 0.0),
                            jnp.where(incl, pick(2 * n, right), 0.0)], axis=0).astype(BF16)
    a_rb = jnp.where(incl, pick(2 * n, left), 0.0).astype(BF16)

    blocks = 2 * n_chunks
    brow = lax.broadcasted_iota(jnp.int32, (2 * n, 2 * n), 0)
    bcol = lax.broadcasted_iota(jnp.int32, (2 * n, 2 * n), 1)
    on_block = (brow // CHUNK) == (bcol // CHUNK)
    spread = lambda x: jnp.where(on_block, jnp.concatenate([x.astype(BF16)] * blocks, axis=0), 0.0)
    steps = int(math.log2(CHUNK)) - 1
    q = a_ab[:CHUNK]
    for c in range(1, n_chunks):
        q = q + a_ab[c * CHUNK:(c + 1) * CHUNK]
    p = _bdot(q, spread(q))
    kr = jnp.dot(a_kr, _heads_stacked(v, m0).astype(BF16), preferred_element_type=F32)
    yield
    akv, rkv = kr[:n], kr[n:]
    for _ in range(steps - 1):
        res = _bdot(jnp.concatenate([p, q], axis=0), spread(p))
        yield
        q = q + p + res[CHUNK:]
        p = res[:CHUNK]
    fin = _bdot(q, spread(p))
    yield
    q = q + p + fin
    q16 = jnp.where(same, jnp.concatenate([q] * n_chunks, axis=0), 0.0).astype(BF16)

    qx = jnp.dot(q16, jnp.concatenate([_heads_stacked(at, m0), _heads_stacked(akv, m0)], axis=1).astype(BF16),
                 preferred_element_type=F32)
    yield
    a_hat = at + qx[:, :LANES]
    uv = akv + qx[:, LANES:]
    both = jnp.dot(a_rb, jnp.concatenate([_heads_stacked(a_hat, m0), _heads_stacked(uv, m0)],
                                         axis=1).astype(BF16), preferred_element_type=F32)
    yield
    r_hat = rt + both[:, :LANES]
    yv = both[:, LANES:] + rkv

    srow = lax.broadcasted_iota(jnp.int32, (LANES, LANES), 0)
    scol = lax.broadcasted_iota(jnp.int32, (LANES, LANES), 1)
    pair_diag = (srow < HEAD) == (scol < HEAD)
    state = st_ref[st_idx]
    for c in (range(n_chunks - 1, -1, -1) if reverse else range(n_chunks)):
        sl = slice(c * CHUNK, (c + 1) * CHUNK)
        proj = _bdot_nt(jnp.concatenate([a_hat[sl], r_hat[sl]], axis=0), state)
        yield
        u_c = proj[:CHUNK] + uv[sl]
        y_ref[0, sl, ls] = proj[CHUNK:] + yv[sl]
        upd = _bdot_tn(jnp.concatenate([u_c, v[sl]], axis=0), jnp.concatenate([bh[sl], kh[sl]], axis=0))
        yield
        state = state * w_chunk[c] + jnp.where(pair_diag, upd, 0.0)
    st_ref[st_idx] = state


def _run_interleaved(chains):
    chains = list(chains)
    while chains:
        alive = []
        for ch in chains:
            try:
                next(ch)
                alive.append(ch)
            except StopIteration:
                pass
        chains = alive


def _scan_kernel(*refs, has_init, want_state, groups):
    fwd_refs, bwd_refs = refs[:6], refs[6:12]
    ka_ref = refs[12]
    pos = 13
    s0_ref = None
    if has_init:
        s0_ref = refs[pos]
        pos += 1
    yf_ref, yb_ref = refs[pos:pos + 2]
    pos += 2
    so_ref = None
    if want_state:
        so_ref = refs[pos]
        pos += 1
    st_ref = refs[pos]
    i = pl.program_id(2)

    @pl.when(i == 0)
    def _():
        if has_init:
            st_ref[...] = s0_ref[0]
        else:
            st_ref[...] = jnp.zeros(st_ref.shape, F32)

    chains = []
    for g in range(groups):
        ls = slice(g * LANES, (g + 1) * LANES)
        k_a = ka_ref[:, ls]
        chains.append(_scan_chain(fwd_refs, ls, k_a, False, st_ref, (0, g), yf_ref))
        chains.append(_scan_chain(bwd_refs, ls, k_a, True, st_ref, (1, g), yb_ref))
    _run_interleaved(chains)

    if want_state:
        @pl.when(i == pl.num_programs(2) - 1)
        def _():
            for z in range(2):
                for g in range(groups):
                    s = st_ref[z, g]
                    so_ref[0, z, 2 * g] = s[:HEAD, :HEAD]
                    so_ref[0, z, 2 * g + 1] = s[HEAD:, HEAD:]


def _rwkv_scan(r, k, v, kk, lw_f, lw_b, a_f, a_b, k_a, s0, want_state):
    b, t, width = r.shape
    pairs = width // LANES
    g = SCAN_PAIRS
    gw = g * LANES
    nt = t // TBLK
    fwd = pl.BlockSpec((1, TBLK, gw), lambda bi, p, i: (bi, i, p))
    bwd = pl.BlockSpec((1, TBLK, gw), lambda bi, p, i: (bi, nt - 1 - i, p))
    st_spec = pl.BlockSpec((1, 2, g, LANES, LANES), lambda bi, p, i: (bi, 0, p, 0, 0))
    specs = [fwd] * 6 + [bwd] * 6 + [pl.BlockSpec((1, gw), lambda bi, p, i: (0, p))]
    ins = [r, k, v, kk, lw_f, a_f, r, k, v, kk, lw_b, a_b, k_a.reshape(1, width)]
    if s0 is not None:
        specs.append(st_spec)
        ins.append(s0)
    y_shape = jax.ShapeDtypeStruct((b, t, width), F32)
    out_shape = [y_shape, y_shape]
    out_specs = [fwd, bwd]
    if want_state:
        out_shape.append(jax.ShapeDtypeStruct((b, 2, 2 * pairs, HEAD, HEAD), F32))
        out_specs.append(pl.BlockSpec((1, 2, 2 * g, HEAD, HEAD), lambda bi, p, i: (bi, 0, p, 0, 0)))
    kern = functools.partial(_scan_kernel, has_init=s0 is not None, want_state=want_state, groups=g)
    return pl.pallas_call(
        kern,
        out_shape=out_shape,
        grid=(b, pairs // g, nt),
        in_specs=specs,
        out_specs=out_specs,
        scratch_shapes=[pltpu.VMEM((2, g, LANES, LANES), F32)],
        compiler_params=_cparams(("arbitrary", "arbitrary", "arbitrary")),
        name="rwkv7_scan",
    )(*ins)


def _pair_states(s):
    b, z, h, n, _ = s.shape
    sp = s.reshape(b, z, h // 2, 2, n, n)
    zero = jnp.zeros_like(sp[:, :, :, 0])
    top = jnp.concatenate([sp[:, :, :, 0], zero], axis=-1)
    bot = jnp.concatenate([zero, sp[:, :, :, 1]], axis=-1)
    return jnp.concatenate([top, bot], axis=-2)


def _post_kernel(yf_ref, yb_ref, r_ref, k_ref, v_ref, gf_ref, gr_ref, g0_ref, g1_ref, fo_ref, x_ref, mod_ref,
                 vec_ref, seg_ref, wf_ref, wr_ref, wo_ref, o_ref, z_ref, *, cw):
    width = yf_ref.shape[2]
    inv_n = 1.0 / HEAD
    for j in range(width // cw):
        sl = slice(j * cw, (j + 1) * cw)
        seg = seg_ref[...]
        y = yf_ref[0, :, sl] + yb_ref[0, :, sl]
        mean = _dot_exact_rhs(y, seg) * inv_n
        d = y - mean
        var = _dot_exact_rhs(d * d, seg) * inv_n
        yn = d * lax.rsqrt(var + GN_EPS) * vec_ref[0:1, sl] + vec_ref[1:2, sl]
        rk = _dot_exact_rhs(r_ref[0, :, sl] * k_ref[0, :, sl] * vec_ref[2:3, sl], seg)
        z = (yn + rk * v_ref[0, :, sl]) * _silu(gr_ref[0, :, sl])
        z_ref[:, sl] = z.astype(BF16)
    out_r = jnp.dot(z_ref[...], wr_ref[...], preferred_element_type=F32)
    zf = (fo_ref[0] * _silu(gf_ref[0])).astype(BF16)
    out_f = jnp.dot(zf, wf_ref[...], preferred_element_type=F32)
    merged = jax.nn.sigmoid(g0_ref[0]) * out_f + jax.nn.sigmoid(g1_ref[0]) * out_r
    o = jnp.dot(merged.astype(BF16), wo_ref[...], preferred_element_type=F32)
    xn = x_ref[0] + mod_ref[0][2:3] * o
    o_ref[0] = xn * lax.rsqrt(jnp.mean(xn * xn, axis=-1, keepdims=True) + RMS_EPS) * vec_ref[3:4, :]


def _post(yf, yb, r, k, v, u, fo, x, mod, lnx_g, lnx_b, r_k, final_g, w_proj_f, w_proj_r, w_out, tm):
    b, t, d = x.shape
    width = yf.shape[2]
    fw = fo.shape[2]
    cw = MXU_DIM
    vec = jnp.stack([lnx_g, lnx_b, r_k.reshape(-1), final_g], axis=0)
    tok = lambda w, c: pl.BlockSpec((1, tm, w), lambda bi, i: (bi, i, c))
    const = lambda shape: pl.BlockSpec(shape, lambda bi, i: (0,) * len(shape))
    gr_off = 2 * fw + 3 * width
    mg_off = gr_off + width
    assert gr_off % width == 0 and mg_off % d == 0
    gr_blk, mg_blk = gr_off // width, mg_off // d
    specs = [tok(width, 0)] * 5 + [tok(fw, 1), tok(width, gr_blk), tok(d, mg_blk), tok(d, mg_blk + 1),
                                   tok(fw, 0), tok(d, 0), pl.BlockSpec((1, 8, d), lambda bi, i: (bi, 0, 0)),
                                   const((4, d)), const((cw, cw)), const((fw, d)), const((width, d)),
                                   const((d, d))]
    return pl.pallas_call(
        functools.partial(_post_kernel, cw=cw),
        out_shape=jax.ShapeDtypeStruct((b, t, d), F32),
        grid=(b, t // tm),
        in_specs=specs,
        out_specs=tok(d, 0),
        scratch_shapes=[pltpu.VMEM((tm, width), BF16)],
        compiler_params=_cparams(("arbitrary", "arbitrary")),
        name="post_mix",
    )(yf, yb, r, k, v, u, u, u, u, fo, x, mod, vec, _seg_matrix(cw),
      w_proj_f.astype(BF16), w_proj_r.astype(BF16), w_out.astype(BF16))


def _mixer_path(x, mod, s0, want_state, grid_shift, p, tiles):
    b, t, d = x.shape
    u = _inproj(x, mod, p["norm_g"], p["w_in"], tiles["inproj"], p["w_in_segments"])
    fo = _fourier(u, t, p["fourier_width"], tiles["fourier_n1"], tiles["fourier"])
    r, k, v, kk, lw_f, lw_b, a_f, a_b = _rwkv_prep(
        u, p["mu_shift"], p["k_k"], p["w_up"], p["a_up"], p["w0"], p["a0"], grid_shift, tiles["prep"])
    res = _rwkv_scan(r, k, v, kk, lw_f, lw_b, a_f, a_b, p["k_a"], s0, want_state)
    y = _post(res[0], res[1], r, k, v, u, fo, x, mod, p["lnx_g"], p["lnx_b"], p["r_k"], p["final_g"],
              p["w_proj_f"], p["w_proj_r"], p["w_out"], tiles["post"])
    return y, (res[2] if want_state else None)


def _tiles(t, grid_shift):
    if not grid_shift:
        return dict(inproj=t, fourier=t, fourier_n1=1, prep=t, post=t)
    return dict(inproj=256, fourier=256, fourier_n1=4, prep=256, post=256)


def kernel(x_prompt, x_sample, state_rwkv, c, c_ctx, norm_g, w_ada, b_ada, w_in, mu_shift, w0, w_up, a0,
           a_up, k_k, k_a, r_k, lnx_g, lnx_b, w_proj_f, w_proj_r, w_out, final_g):
    depth = w_in.shape[0]
    assert depth == 1, "the final norm is fused into the single layer's post kernel"
    bp, tp, d = x_prompt.shape
    bs, ts, _ = x_sample.shape
    width = k_k.shape[1]
    fw = w_proj_f.shape[1]
    l = 0
    sh_end = 2 * fw + 3 * width
    rank2 = mu_shift.shape[1] - 3 * width
    n_in = w_in.shape[2]
    segments = ((0, 0, sh_end), (sh_end + rank2, sh_end, n_in - sh_end - rank2), (sh_end, n_in - rank2, rank2))
    p = dict(norm_g=norm_g[l], w_in=w_in[l].astype(BF16), w_in_segments=segments, fourier_width=fw,
             mu_shift=mu_shift[l], k_k=k_k[l], w_up=w_up[l], a_up=a_up[l], w0=w0[l], a0=a0[l], k_a=k_a[l],
             lnx_g=lnx_g[l], lnx_b=lnx_b[l], r_k=r_k[l], final_g=final_g, w_proj_f=w_proj_f[l],
             w_proj_r=w_proj_r[l], w_out=w_out[l])

    rows = 8
    cvec = jnp.concatenate([c_ctx[None], c, jnp.zeros((rows - 1 - bs, d), F32)], axis=0)
    m = _modulation(cvec, w_ada[l], b_ada[l])
    m3 = m.reshape(rows, 3, d)
    mod = jnp.concatenate([m3, jnp.zeros((rows, 5, d), F32)], axis=1)

    yp, sp = _mixer_path(x_prompt, jnp.broadcast_to(mod[0:1], (bp, 8, d)), None, True, False, p,
                         _tiles(tp, False))
    s0 = _pair_states(state_rwkv[:, l])
    ys, _ = _mixer_path(x_sample, mod[1:1 + bs], s0, False, True, p, _tiles(ts, True))
    new_state = sp[:, None]
    return yp, ys, new_state
```

```python
import functools
import math

import numpy as np
import jax
import jax.numpy as jnp
from jax import lax
from jax.experimental import pallas as pl
from jax.experimental.pallas import tpu as pltpu

F32 = jnp.float32
BF16 = jnp.bfloat16

LANES = 128
MXU_DIM = 256
VMEM_LIMIT = 56 * 1024 * 1024

HEAD = 64
GRID_W = 64
FOURIER_GROUP = 128
RMS_EPS = 1e-6
GN_EPS = 64e-5

TBLK = 128
CHUNK = 32
SCAN_PAIRS = 8


def _cparams(sem):
    return pltpu.CompilerParams(dimension_semantics=sem, vmem_limit_bytes=VMEM_LIMIT)


def _bdot(a, b):
    return jnp.dot(a.astype(BF16), b.astype(BF16), preferred_element_type=F32)


def _bdot_nt(a, b):
    return lax.dot_general(a.astype(BF16), b.astype(BF16), (((1,), (1,)), ((), ())),
                           preferred_element_type=F32)


def _bdot_tn(a, b):
    return lax.dot_general(a.astype(BF16), b.astype(BF16), (((0,), (0,)), ((), ())),
                           preferred_element_type=F32)


def _split2(x):
    hi = x.astype(BF16)
    lo = (x - hi.astype(F32)).astype(BF16)
    return hi, lo


def _dot_exact_rhs(x, m):
    hi, lo = _split2(x)
    return (jnp.dot(hi, m, preferred_element_type=F32) + jnp.dot(lo, m, preferred_element_type=F32))


def _silu(x):
    return x * jax.nn.sigmoid(x)


def _mod_kernel(c_ref, w_ref, b_ref, o_ref):
    c = c_ref[...]
    o_ref[...] = jnp.dot(_silu(c), w_ref[...], precision=lax.Precision.HIGHEST,
                         preferred_element_type=F32) + b_ref[...]


def _modulation(cvec, w_ada, b_ada):
    rows, d = cvec.shape
    n = w_ada.shape[1]
    tn = 512
    return pl.pallas_call(
        _mod_kernel,
        out_shape=jax.ShapeDtypeStruct((rows, n), F32),
        grid=(n // tn,),
        in_specs=[pl.BlockSpec((rows, d), lambda j: (0, 0)),
                  pl.BlockSpec((d, tn), lambda j: (0, j)),
                  pl.BlockSpec((1, tn), lambda j: (0, j))],
        out_specs=pl.BlockSpec((rows, tn), lambda j: (0, j)),
        compiler_params=_cparams(("arbitrary",)),
        name="adaln_modulation",
    )(cvec, w_ada, b_ada.reshape(1, n))


def _inproj_kernel(x_ref, mod_ref, g_ref, w_ref, o_ref, *, tn, segments):
    x = x_ref[0]
    y = x * lax.rsqrt(jnp.mean(x * x, axis=-1, keepdims=True) + RMS_EPS) * g_ref[...]
    m = mod_ref[0]
    h = (y * (1.0 + m[1:2]) + m[0:1]).astype(BF16)
    for src, dst, size in segments:
        for j0 in range(0, size, tn):
            w = min(tn, size - j0)
            o_ref[0, :, dst + j0:dst + j0 + w] = jnp.dot(h, w_ref[:, src + j0:src + j0 + w],
                                                         preferred_element_type=F32)


def _inproj(x, mod, norm_g, w_bf16, tm, segments):
    b, t, d = x.shape
    n = w_bf16.shape[1]
    return pl.pallas_call(
        functools.partial(_inproj_kernel, tn=2 * MXU_DIM, segments=segments),
        out_shape=jax.ShapeDtypeStruct((b, t, n), F32),
        grid=(b, t // tm),
        in_specs=[pl.BlockSpec((1, tm, d), lambda bi, i: (bi, i, 0)),
                  pl.BlockSpec((1, 8, d), lambda bi, i: (bi, 0, 0)),
                  pl.BlockSpec((1, d), lambda bi, i: (0, 0)),
                  pl.BlockSpec((d, n), lambda bi, i: (0, 0), pipeline_mode=pl.Buffered(1))],
        out_specs=pl.BlockSpec((1, tm, n), lambda bi, i: (bi, i, 0)),
        compiler_params=_cparams(("arbitrary", "arbitrary")),
        name="norm_mod_inproj",
    )(x, mod, norm_g.reshape(1, d), w_bf16)


_ROT = ((1.0, 0, -1.0, 1), (-1.0, 1, -1.0, 0), (-1.0, 0, 1.0, 1), (1.0, 1, 1.0, 0))


def _fourier_kernel(xf_ref, cc_ref, er_ref, ei_ref, o_ref, g_ref, il_ref, *, n1, t2, rows):
    width = xf_ref.shape[2]

    @pl.when(pl.program_id(1) == 0)
    def _():
        def body(i, carry):
            r0 = pl.multiple_of(i * rows, rows)
            pq = []
            for a in range(n1):
                x = xf_ref[0, pl.ds(a * t2 + r0, rows), :]
                parts = [_bdot(x[:, g * FOURIER_GROUP:(g + 1) * FOURIER_GROUP], cc_ref[...])
                         for g in range(width // FOURIER_GROUP)]
                pq.append((jnp.concatenate([p[:, :FOURIER_GROUP] for p in parts], axis=1),
                           jnp.concatenate([p[:, FOURIER_GROUP:] for p in parts], axis=1)))
            for f1 in range(n1):
                re = im = None
                for a in range(n1):
                    sr, cr, si, ci = _ROT[(a * f1) % 4]
                    tr, ti = sr * pq[a][cr], si * pq[a][ci]
                    re = tr if re is None else re + tr
                    im = ti if im is None else im + ti
                g_ref[f1, pl.ds(r0, rows), :] = re.astype(BF16)
                g_ref[f1, pl.ds(t2 + r0, rows), :] = im.astype(BF16)
            return carry
        lax.fori_loop(0, t2 // rows, body, 0)

    tf = er_ref.shape[1]
    for f1 in range(n1):
        res = (jnp.dot(er_ref[f1], g_ref[f1, 0:t2], preferred_element_type=F32)
               + jnp.dot(ei_ref[f1], g_ref[f1, t2:2 * t2], preferred_element_type=F32))
        if n1 == 1:
            o_ref[0] = res
        else:
            for g in range(width // LANES):
                il_ref[g, pl.ds(f1, tf, stride=n1), :] = res[:, g * LANES:(g + 1) * LANES]
    if n1 > 1:
        for g in range(width // LANES):
            o_ref[0, :, g * LANES:(g + 1) * LANES] = il_ref[g]


def _dft_consts(t, n1):
    t2 = t // n1
    g = FOURIER_GROUP
    j = np.arange(g)
    ang = 2.0 * np.pi * ((j[:, None] * j[None, :]) % g) / g
    cc = jnp.asarray(np.concatenate([np.cos(ang), np.sin(ang)], axis=1) / math.sqrt(g), dtype=F32).astype(BF16)
    scale = 1.0 / math.sqrt(t)
    if n1 == 1:
        k = np.arange(t)
        ang = 2.0 * np.pi * ((k[:, None] * k[None, :]) % t) / t
        tab = lambda m: jnp.asarray(m[None] * scale, dtype=F32).astype(BF16)
        return tab(np.cos(ang)), tab(np.sin(ang)), cc
    sa = 1 << (int(math.log2(t2)) // 2)
    sb = t2 // sa
    assert sa * sb == t2 and n1 * t2 == t
    tt = np.arange(t2)
    f1 = np.arange(n1)[:, None, None]
    bb = np.arange(sb)[None, :, None]
    ang_x = 2.0 * np.pi * ((tt[None, None, :] * (f1 + n1 * bb)) % t) / t
    aa = np.arange(sa)[:, None]
    ang_y = 2.0 * np.pi * ((tt[None, :] * aa) % sa) / sa
    xr, xi = jnp.asarray(np.cos(ang_x) * scale, F32), jnp.asarray(-np.sin(ang_x) * scale, F32)
    yr, yi = jnp.asarray(np.cos(ang_y), F32), jnp.asarray(-np.sin(ang_y), F32)
    xr, xi = xr[:, None], xi[:, None]
    yr, yi = yr[None, :, None], yi[None, :, None]
    re = (xr * yr - xi * yi).reshape(n1, t2, t2)
    im = (xr * yi + xi * yr).reshape(n1, t2, t2)
    return re.astype(BF16), (-im).astype(BF16), cc


def _fourier(u, t, width, n1, tf):
    b = u.shape[0]
    t2 = t // n1
    e_re, e_nim, cc = _dft_consts(t, n1)
    e_spec = pl.BlockSpec((n1, tf, t2), lambda bi, i: (0, i, 0))
    kern = functools.partial(_fourier_kernel, n1=n1, t2=t2, rows=min(t2, 256))
    return pl.pallas_call(
        kern,
        out_shape=jax.ShapeDtypeStruct((b, t, width), F32),
        grid=(b, t2 // tf),
        in_specs=[pl.BlockSpec((1, t, width), lambda bi, i: (bi, 0, 0)),
                  pl.BlockSpec((FOURIER_GROUP, 2 * FOURIER_GROUP), lambda bi, i: (0, 0)),
                  e_spec, e_spec],
        out_specs=pl.BlockSpec((1, n1 * tf, width), lambda bi, i: (bi, i, 0)),
        scratch_shapes=[pltpu.VMEM((n1, 2 * t2, width), BF16),
                        pltpu.VMEM((width // LANES, n1 * tf if n1 > 1 else 8, LANES), F32)],
        compiler_params=_cparams(("arbitrary", "arbitrary")),
        name="fourier_mix",
    )(u, cc, e_re, e_nim)


def _shift(x, prev_halo, next_halo, t_total, row0, grid_shift):
    rows, lanes = x.shape
    row = lax.broadcasted_iota(jnp.int32, (rows, lanes), 0)
    lane = lax.broadcasted_iota(jnp.int32, (rows, lanes), 1)
    before = pltpu.roll(x, 1, 0)
    after = pltpu.roll(x, rows - 1, 0)
    if not grid_shift:
        prev = jnp.where(row == 0, 0.0, before)
        nxt = jnp.where(row == rows - 1, 0.0, after)
        return jnp.where((lane & 1) == 0, prev, nxt)
    col = row & (GRID_W - 1)
    grow = row + row0
    left = jnp.where(col == 0, 0.0, before)
    right = jnp.where(col == GRID_W - 1, 0.0, after)
    up = jnp.concatenate([prev_halo, x[:rows - GRID_W]], axis=0)
    up = jnp.where(grow < GRID_W, 0.0, up)
    down = jnp.concatenate([x[GRID_W:], next_halo], axis=0)
    down = jnp.where(grow >= t_total - GRID_W, 0.0, down)
    m = lane & 3
    return jnp.where(m == 0, left, jnp.where(m == 1, right, jnp.where(m == 2, up, down)))


def _cumsum_matrices(rows):
    t = np.arange(rows)
    same = (t[:, None] // CHUNK) == (t[None, :] // CHUNK)
    tri = np.stack([same & (t[None, :] <= t[:, None]), same & (t[None, :] >= t[:, None])])
    return jnp.asarray(tri.astype(np.float32), dtype=BF16)


def _prep_kernel(*refs, grid_shift, t_total, width, cw):
    if grid_shift:
        (r_ref, k_ref, v_ref, d_ref, rp_ref, kp_ref, vp_ref, dp_ref, rn_ref, kn_ref, vn_ref, dn_ref,
         mu_ref, mud_ref, kk_ref, wcat_ref, bias_ref, seg_ref, tri_ref,
         ro_ref, ko_ref, vo_ref, kko_ref, lwf_ref, lwb_ref, af_ref, ab_ref) = refs
    else:
        (r_ref, k_ref, v_ref, d_ref, mu_ref, mud_ref, kk_ref, wcat_ref, bias_ref, seg_ref, tri_ref,
         ro_ref, ko_ref, vo_ref, kko_ref, lwf_ref, lwb_ref, af_ref, ab_ref) = refs
        rp_ref = kp_ref = vp_ref = dp_ref = rn_ref = kn_ref = vn_ref = dn_ref = None
    rows = r_ref.shape[1]
    row0 = pl.program_id(1) * rows

    def lerp(x_ref, p_ref, n_ref, mu, sl):
        x = x_ref[0, :, sl]
        ph = p_ref[0, :, sl] if grid_shift else None
        nh = n_ref[0, :, sl] if grid_shift else None
        s = _shift(x, ph, nh, t_total, row0, grid_shift)
        return x + mu * (s - x)

    dsl = slice(0, LANES)
    d = lerp(d_ref, dp_ref, dn_ref, mud_ref[...], dsl)
    lane = lax.broadcasted_iota(jnp.int32, d.shape, 1)
    d = jnp.where(lane < HEAD, jnp.tanh(d), d).astype(BF16)
    outs = (lwf_ref, lwb_ref, af_ref, ab_ref)
    for j in range(width // cw):
        sl = slice(j * cw, (j + 1) * cw)
        ro_ref[0, :, sl] = lerp(r_ref, rp_ref, rn_ref, mu_ref[0:1, sl], sl)
        vo_ref[0, :, sl] = lerp(v_ref, vp_ref, vn_ref, mu_ref[2:3, sl], sl)
        k = lerp(k_ref, kp_ref, kn_ref, mu_ref[1:2, sl], sl)
        ko_ref[0, :, sl] = k
        kk = k * kk_ref[0:1, sl]
        ss = _dot_exact_rhs(kk * kk, seg_ref[...])
        kko_ref[0, :, sl] = kk / jnp.maximum(jnp.sqrt(ss), 1e-12)
        for z in range(4):
            wsl = slice(z * width + j * cw, z * width + (j + 1) * cw)
            pre = jnp.dot(d, wcat_ref[:, wsl], preferred_element_type=F32) + bias_ref[z:z + 1, sl]
            sg = jax.nn.sigmoid(pre)
            if z < 2:
                hi, lo = _split2(sg * (-math.exp(-0.5)))
                outs[z][0, :, sl] = (jnp.dot(tri_ref[z], hi, preferred_element_type=F32)
                                     + jnp.dot(tri_ref[z], lo, preferred_element_type=F32))
            else:
                outs[z][0, :, sl] = sg


def _rwkv_prep(u, mu_shift, k_k, w_up, a_up, w0, a0, grid_shift, tt):
    b, t, _ = u.shape
    width = k_k.shape[0]
    cw = MXU_DIM
    mu = mu_shift[:3 * width].reshape(3, width)
    mud = mu_shift[3 * width:].reshape(1, LANES)
    zpad = jnp.zeros((HEAD, width), F32)
    wcat = jnp.concatenate([jnp.concatenate([w_up[0], zpad], 0), jnp.concatenate([w_up[1], zpad], 0),
                            jnp.concatenate([zpad, a_up[0]], 0), jnp.concatenate([zpad, a_up[1]], 0)],
                           axis=1).astype(BF16)
    bias = jnp.concatenate([w0, a0], axis=0)
    seg = _seg_matrix(cw)
    d_blk = (u.shape[2] - LANES) // LANES
    main = [pl.BlockSpec((1, tt, width), lambda bi, i, c=c: (bi, i, c)) for c in (1, 2, 3)]
    main.append(pl.BlockSpec((1, tt, LANES), lambda bi, i: (bi, i, d_blk)))
    ins = [u, u, u, u]
    specs = list(main)
    if grid_shift:
        hb = tt // GRID_W
        last = t // GRID_W - 1
        for off in (-1, hb):
            def imap(bi, i, c, off=off):
                return (bi, jnp.clip(i * hb + off, 0, last), c)
            specs += [pl.BlockSpec((1, GRID_W, width), functools.partial(imap, c=c)) for c in (1, 2, 3)]
            specs.append(pl.BlockSpec((1, GRID_W, LANES), functools.partial(imap, c=d_blk)))
            ins += [u, u, u, u]
    const = lambda shape: pl.BlockSpec(shape, lambda bi, i: (0, 0))
    specs += [const((3, width)), const((1, LANES)), const((1, width)), const((LANES, 4 * width)),
              const((4, width)), const((cw, cw)), pl.BlockSpec((2, tt, tt), lambda bi, i: (0, 0, 0))]
    ins += [mu, mud, k_k.reshape(1, width), wcat, bias, seg, _cumsum_matrices(tt)]
    out_spec = pl.BlockSpec((1, tt, width), lambda bi, i: (bi, i, 0))
    kern = functools.partial(_prep_kernel, grid_shift=grid_shift, t_total=t, width=width, cw=cw)
    return pl.pallas_call(
        kern,
        out_shape=[jax.ShapeDtypeStruct((b, t, width), F32)] * 8,
        grid=(b, t // tt),
        in_specs=specs,
        out_specs=[out_spec] * 8,
        compiler_params=_cparams(("arbitrary", "arbitrary")),
        name="rwkv_prep",
    )(*ins)


def _seg_matrix(n):
    i = np.arange(n) // HEAD
    return jnp.asarray((i[:, None] == i[None, :]).astype(np.float32), dtype=BF16)


def _heads_stacked(x, m0):
    return jnp.concatenate([jnp.where(m0, x, 0.0), jnp.where(m0, 0.0, x)], axis=0)


def _scan_chain(in_refs, ls, k_a, reverse, st_ref, st_idx, y_ref):
    r_ref, k_ref, v_ref, kk_ref, cl_ref, a_ref = in_refs
    n = TBLK
    cl = cl_ref[0, :, ls]
    pos = lax.broadcasted_iota(jnp.int32, (n, LANES), 0) & (CHUNK - 1)
    if reverse:
        cl_ex = jnp.where(pos == CHUNK - 1, 0.0, pltpu.roll(cl, n - 1, 0))
    else:
        cl_ex = jnp.where(pos == 0, 0.0, pltpu.roll(cl, 1, 0))
    n_chunks = n // CHUNK
    ends = [c * CHUNK if reverse else (c + 1) * CHUNK - 1 for c in range(n_chunks)]
    tot_rows = [cl[e:e + 1] for e in ends]
    tot = jnp.concatenate([jnp.broadcast_to(tr, (CHUNK, LANES)) for tr in tot_rows], axis=0)
    r, k, v, kk, a = r_ref[0, :, ls], k_ref[0, :, ls], v_ref[0, :, ls], kk_ref[0, :, ls], a_ref[0, :, ls]
    kd = k * (1.0 + (a - 1.0) * k_a)
    b = kk * a
    at = -kk * jnp.exp(cl_ex)
    rt = r * jnp.exp(cl)
    e_neg = jnp.exp(-cl)
    e_tail = jnp.exp(tot - cl)
    bh = b * e_tail
    kh = kd * e_tail
    w_chunk = [jnp.exp(tr) for tr in tot_rows]

    m0 = lax.broadcasted_iota(jnp.int32, (n, LANES), 1) < HEAD
    lhs = jnp.concatenate([_heads_stacked(at, m0), _heads_stacked(rt, m0)], axis=0)
    rhs = jnp.concatenate([b * e_neg, kd * e_neg], axis=0)
    g = _bdot_nt(lhs, rhs)
    yield

    row = lax.broadcasted_iota(jnp.int32, (n, 2 * n), 0)
    col = lax.broadcasted_iota(jnp.int32, (n, 2 * n), 1) & (n - 1)
    same = (row // CHUNK) == (col // CHUNK)
    if reverse:
        strict, incl = same & (col > row), same & (col >= row)
    else:
        strict, incl = same & (col < row), same & (col <= row)
    pick = lambda r0, cs: jnp.concatenate([g[r0:r0 + n, cs], g[r0 + n:r0 + 2 * n, cs]], axis=1)
    left, right = slice(0, n), slice(n, 2 * n)
    a_ab = jnp.where(strict, pick(0, left), 0.0)
    a_kr = jnp.concatenate([jnp.where(strict, pick(0, right), 0.0),
                            jnp.where(incl, pick(2 * n, right), 0.0)], axis=0).astype(BF16)
    a_rb = jnp.where(incl, pick(2 * n, left), 0.0).astype(BF16)

    blocks = 2 * n_chunks
    brow = lax.broadcasted_iota(jnp.int32, (2 * n, 2 * n), 0)
    bcol = lax.broadcasted_iota(jnp.int32, (2 * n, 2 * n), 1)
    on_block = (brow // CHUNK) == (bcol // CHUNK)
    spread = lambda x: jnp.where(on_block, jnp.concatenate([x.astype(BF16)] * blocks, axis=0), 0.0)
    steps = int(math.log2(CHUNK)) - 1
    q = a_ab[:CHUNK]
    for c in range(1, n_chunks):
        q = q + a_ab[c * CHUNK:(c + 1) * CHUNK]
    p = _bdot(q, spread(q))
    kr = jnp.dot(a_kr, _heads_stacked(v, m0).astype(BF16), preferred_element_type=F32)
    yield
    akv, rkv = kr[:n], kr[n:]
    for _ in range(steps - 1):
        res = _bdot(jnp.concatenate([p, q], axis=0), spread(p))
        yield
        q = q + p + res[CHUNK:]
        p = res[:CHUNK]
    fin = _bdot(q, spread(p))
    yield
    q = q + p + fin
    q16 = jnp.where(same, jnp.concatenate([q] * n_chunks, axis=0), 0.0).astype(BF16)

    qx = jnp.dot(q16, jnp.concatenate([_heads_stacked(at, m0), _heads_stacked(akv, m0)], axis=1).astype(BF16),
                 preferred_element_type=F32)
    yield
    a_hat = at + qx[:, :LANES]
    uv = akv + qx[:, LANES:]
    both = jnp.dot(a_rb, jnp.concatenate([_heads_stacked(a_hat, m0), _heads_stacked(uv, m0)],
                                         axis=1).astype(BF16), preferred_element_type=F32)
    yield
    r_hat = rt + both[:, :LANES]
    yv = both[:, LANES:] + rkv

    srow = lax.broadcasted_iota(jnp.int32, (LANES, LANES), 0)
    scol = lax.broadcasted_iota(jnp.int32, (LANES, LANES), 1)
    pair_diag = (srow < HEAD) == (scol < HEAD)
    state = st_ref[st_idx]
    for c in (range(n_chunks - 1, -1, -1) if reverse else range(n_chunks)):
        sl = slice(c * CHUNK, (c + 1) * CHUNK)
        proj = _bdot_nt(jnp.concatenate([a_hat[sl], r_hat[sl]], axis=0), state)
        yield
        u_c = proj[:CHUNK] + uv[sl]
        y_ref[0, sl, ls] = proj[CHUNK:] + yv[sl]
        upd = _bdot_tn(jnp.concatenate([u_c, v[sl]], axis=0), jnp.concatenate([bh[sl], kh[sl]], axis=0))
        yield
        state = state * w_chunk[c] + jnp.where(pair_diag, upd, 0.0)
    st_ref[st_idx] = state


def _run_interleaved(chains):
    chains = list(chains)
    while chains:
        alive = []
        for ch in chains:
            try:
                next(ch)
                alive.append(ch)
            except StopIteration:
                pass
        chains = alive


def _scan_kernel(*refs, has_init, want_state, groups):
    fwd_refs, bwd_refs = refs[:6], refs[6:12]
    ka_ref = refs[12]
    pos = 13
    s0_ref = None
    if has_init:
        s0_ref = refs[pos]
        pos += 1
    yf_ref, yb_ref = refs[pos:pos + 2]
    pos += 2
    so_ref = None
    if want_state:
        so_ref = refs[pos]
        pos += 1
    st_ref = refs[pos]
    i = pl.program_id(2)

    @pl.when(i == 0)
    def _():
        if has_init:
            st_ref[...] = s0_ref[0]
        else:
            st_ref[...] = jnp.zeros(st_ref.shape, F32)

    chains = []
    for g in range(groups):
        ls = slice(g * LANES, (g + 1) * LANES)
        k_a = ka_ref[:, ls]
        chains.append(_scan_chain(fwd_refs, ls, k_a, False, st_ref, (0, g), yf_ref))
        chains.append(_scan_chain(bwd_refs, ls, k_a, True, st_ref, (1, g), yb_ref))
    _run_interleaved(chains)

    if want_state:
        @pl.when(i == pl.num_programs(2) - 1)
        def _():
            for z in range(2):
                for g in range(groups):
                    s = st_ref[z, g]
                    so_ref[0, z, 2 * g] = s[:HEAD, :HEAD]
                    so_ref[0, z, 2 * g + 1] = s[HEAD:, HEAD:]


def _rwkv_scan(r, k, v, kk, cl_f, cl_b, a_f, a_b, k_a, s0, want_state):
    b, t, width = r.shape
    pairs = width // LANES
    g = SCAN_PAIRS
    gw = g * LANES
    nt = t // TBLK
    fwd = pl.BlockSpec((1, TBLK, gw), lambda bi, p, i: (bi, i, p))
    bwd = pl.BlockSpec((1, TBLK, gw), lambda bi, p, i: (bi, nt - 1 - i, p))
    st_spec = pl.BlockSpec((1, 2, g, LANES, LANES), lambda bi, p, i: (bi, 0, p, 0, 0))
    specs = [fwd] * 6 + [bwd] * 6 + [pl.BlockSpec((1, gw), lambda bi, p, i: (0, p))]
    ins = [r, k, v, kk, cl_f, a_f, r, k, v, kk, cl_b, a_b, k_a.reshape(1, width)]
    if s0 is not None:
        specs.append(st_spec)
        ins.append(s0)
    y_shape = jax.ShapeDtypeStruct((b, t, width), F32)
    out_shape = [y_shape, y_shape]
    out_specs = [fwd, bwd]
    if want_state:
        out_shape.append(jax.ShapeDtypeStruct((b, 2, 2 * pairs, HEAD, HEAD), F32))
        out_specs.append(pl.BlockSpec((1, 2, 2 * g, HEAD, HEAD), lambda bi, p, i: (bi, 0, p, 0, 0)))
    kern = functools.partial(_scan_kernel, has_init=s0 is not None, want_state=want_state, groups=g)
    return pl.pallas_call(
        kern,
        out_shape=out_shape,
        grid=(b, pairs // g, nt),
        in_specs=specs,
        out_specs=out_specs,
        scratch_shapes=[pltpu.VMEM((2, g, LANES, LANES), F32)],
        compiler_params=_cparams(("arbitrary", "arbitrary", "arbitrary")),
        name="rwkv7_scan",
    )(*ins)


def _pair_states(s):
    b, z, h, n, _ = s.shape
    sp = s.reshape(b, z, h // 2, 2, n, n)
    zero = jnp.zeros_like(sp[:, :, :, 0])
    top = jnp.concatenate([sp[:, :, :, 0], zero], axis=-1)
    bot = jnp.concatenate([zero, sp[:, :, :, 1]], axis=-1)
    return jnp.concatenate([top, bot], axis=-2)


def _post_kernel(yf_ref, yb_ref, r_ref, k_ref, v_ref, gf_ref, gr_ref, g0_ref, g1_ref, fo_ref, x_ref, mod_ref,
                 vec_ref, seg_ref, wf_ref, wr_ref, wo_ref, o_ref, z_ref, *, cw):
    width = yf_ref.shape[2]
    inv_n = 1.0 / HEAD
    for j in range(width // cw):
        sl = slice(j * cw, (j + 1) * cw)
        seg = seg_ref[...]
        y = yf_ref[0, :, sl] + yb_ref[0, :, sl]
        mean = _dot_exact_rhs(y, seg) * inv_n
        d = y - mean
        var = _dot_exact_rhs(d * d, seg) * inv_n
        yn = d * lax.rsqrt(var + GN_EPS) * vec_ref[0:1, sl] + vec_ref[1:2, sl]
        rk = _dot_exact_rhs(r_ref[0, :, sl] * k_ref[0, :, sl] * vec_ref[2:3, sl], seg)
        z = (yn + rk * v_ref[0, :, sl]) * _silu(gr_ref[0, :, sl])
        z_ref[:, sl] = z.astype(BF16)
    out_r = jnp.dot(z_ref[...], wr_ref[...], preferred_element_type=F32)
    zf = (fo_ref[0] * _silu(gf_ref[0])).astype(BF16)
    out_f = jnp.dot(zf, wf_ref[...], preferred_element_type=F32)
    merged = jax.nn.sigmoid(g0_ref[0]) * out_f + jax.nn.sigmoid(g1_ref[0]) * out_r
    o = jnp.dot(merged.astype(BF16), wo_ref[...], preferred_element_type=F32)
    xn = x_ref[0] + mod_ref[0][2:3] * o
    o_ref[0] = xn * lax.rsqrt(jnp.mean(xn * xn, axis=-1, keepdims=True) + RMS_EPS) * vec_ref[3:4, :]


def _post(yf, yb, r, k, v, u, fo, x, mod, lnx_g, lnx_b, r_k, final_g, w_proj_f, w_proj_r, w_out, tm):
    b, t, d = x.shape
    width = yf.shape[2]
    fw = fo.shape[2]
    cw = MXU_DIM
    vec = jnp.stack([lnx_g, lnx_b, r_k.reshape(-1), final_g], axis=0)
    tok = lambda w, c: pl.BlockSpec((1, tm, w), lambda bi, i: (bi, i, c))
    const = lambda shape: pl.BlockSpec(shape, lambda bi, i: (0,) * len(shape))
    gr_off = 2 * fw + 3 * width
    mg_off = gr_off + width
    assert gr_off % width == 0 and mg_off % d == 0
    gr_blk, mg_blk = gr_off // width, mg_off // d
    specs = [tok(width, 0)] * 5 + [tok(fw, 1), tok(width, gr_blk), tok(d, mg_blk), tok(d, mg_blk + 1),
                                   tok(fw, 0), tok(d, 0), pl.BlockSpec((1, 8, d), lambda bi, i: (bi, 0, 0)),
                                   const((4, d)), const((cw, cw)), const((fw, d)), const((width, d)),
                                   const((d, d))]
    return pl.pallas_call(
        functools.partial(_post_kernel, cw=cw),
        out_shape=jax.ShapeDtypeStruct((b, t, d), F32),
        grid=(b, t // tm),
        in_specs=specs,
        out_specs=tok(d, 0),
        scratch_shapes=[pltpu.VMEM((tm, width), BF16)],
        compiler_params=_cparams(("arbitrary", "arbitrary")),
        name="post_mix",
    )(yf, yb, r, k, v, u, u, u, u, fo, x, mod, vec, _seg_matrix(cw),
      w_proj_f.astype(BF16), w_proj_r.astype(BF16), w_out.astype(BF16))


def _mixer_path(x, mod, s0, want_state, grid_shift, p, tiles):
    b, t, d = x.shape
    u = _inproj(x, mod, p["norm_g"], p["w_in"], tiles["inproj"], p["w_in_segments"])
    fo = _fourier(u, t, p["fourier_width"], tiles["fourier_n1"], tiles["fourier"])
    r, k, v, kk, cl_f, cl_b, a_f, a_b = _rwkv_prep(
        u, p["mu_shift"], p["k_k"], p["w_up"], p["a_up"], p["w0"], p["a0"], grid_shift, tiles["prep"])
    res = _rwkv_scan(r, k, v, kk, cl_f, cl_b, a_f, a_b, p["k_a"], s0, want_state)
    y = _post(res[0], res[1], r, k, v, u, fo, x, mod, p["lnx_g"], p["lnx_b"], p["r_k"], p["final_g"],
              p["w_proj_f"], p["w_proj_r"], p["w_out"], tiles["post"])
    return y, (res[2] if want_state else None)


def _tiles(t, grid_shift):
    if not grid_shift:
        return dict(inproj=t, fourier=t, fourier_n1=1, prep=t, post=t)
    return dict(inproj=256, fourier=256, fourier_n1=4, prep=256, post=256)


def kernel(x_prompt, x_sample, state_rwkv, c, c_ctx, norm_g, w_ada, b_ada, w_in, mu_shift, w0, w_up, a0,
           a_up, k_k, k_a, r_k, lnx_g, lnx_b, w_proj_f, w_proj_r, w_out, final_g):
    depth = w_in.shape[0]
    assert depth == 1, "the final norm is fused into the single layer's post kernel"
    bp, tp, d = x_prompt.shape
    bs, ts, _ = x_sample.shape
    width = k_k.shape[1]
    fw = w_proj_f.shape[1]
    l = 0
    sh_end = 2 * fw + 3 * width
    rank2 = mu_shift.shape[1] - 3 * width
    n_in = w_in.shape[2]
    segments = ((0, 0, sh_end), (sh_end + rank2, sh_end, n_in - sh_end - rank2), (sh_end, n_in - rank2, rank2))
    p = dict(norm_g=norm_g[l], w_in=w_in[l].astype(BF16), w_in_segments=segments, fourier_width=fw,
             mu_shift=mu_shift[l], k_k=k_k[l], w_up=w_up[l], a_up=a_up[l], w0=w0[l], a0=a0[l], k_a=k_a[l],
             lnx_g=lnx_g[l], lnx_b=lnx_b[l], r_k=r_k[l], final_g=final_g, w_proj_f=w_proj_f[l],
             w_proj_r=w_proj_r[l], w_out=w_out[l])

    rows = 8
    cvec = jnp.concatenate([c_ctx[None], c, jnp.zeros((rows - 1 - bs, d), F32)], axis=0)
    m = _modulation(cvec, w_ada[l], b_ada[l])
    m3 = m.reshape(rows, 3, d)
    mod = jnp.concatenate([m3, jnp.zeros((rows, 5, d), F32)], axis=1)

    yp, sp = _mixer_path(x_prompt, jnp.broadcast_to(mod[0:1], (bp, 8, d)), None, True, False, p,
                         _tiles(tp, False))
    s0 = _pair_states(state_rwkv[:, l])
    ys, _ = _mixer_path(x_sample, mod[1:1 + bs], s0, False, True, p, _tiles(ts, True))
    new_state = sp[:, None]
    return yp, ys, new_state
```

```python
import functools
import math

import numpy as np
import jax
import jax.numpy as jnp
from jax import lax
from jax.experimental import pallas as pl
from jax.experimental.pallas import tpu as pltpu

F32 = jnp.float32
BF16 = jnp.bfloat16

LANES = 128
MXU_DIM = 256
VMEM_LIMIT = 56 * 1024 * 1024

HEAD = 64
GRID_W = 64
FOURIER_GROUP = 128
RMS_EPS = 1e-6
GN_EPS = 64e-5

TBLK = 128
CHUNK = 32
SCAN_PAIRS = 8
SCAN_BLOCKS = 2
STATE_FREE_ROUNDS = int(math.log2(CHUNK)) + 3


def _cparams(sem):
    return pltpu.CompilerParams(dimension_semantics=sem, vmem_limit_bytes=VMEM_LIMIT)


def _bdot(a, b):
    return jnp.dot(a.astype(BF16), b.astype(BF16), preferred_element_type=F32)


def _bdot_nt(a, b):
    return lax.dot_general(a.astype(BF16), b.astype(BF16), (((1,), (1,)), ((), ())),
                           preferred_element_type=F32)


def _bdot_tn(a, b):
    return lax.dot_general(a.astype(BF16), b.astype(BF16), (((0,), (0,)), ((), ())),
                           preferred_element_type=F32)


def _split2(x):
    hi = x.astype(BF16)
    lo = (x - hi.astype(F32)).astype(BF16)
    return hi, lo


def _dot_exact_rhs(x, m):
    hi, lo = _split2(x)
    return (jnp.dot(hi, m, preferred_element_type=F32) + jnp.dot(lo, m, preferred_element_type=F32))


def _silu(x):
    return x * jax.nn.sigmoid(x)


def _mod_kernel(c_ref, w_ref, b_ref, o_ref):
    c = c_ref[...]
    o_ref[...] = jnp.dot(_silu(c), w_ref[...], precision=lax.Precision.HIGHEST,
                         preferred_element_type=F32) + b_ref[...]


def _modulation(cvec, w_ada, b_ada):
    rows, d = cvec.shape
    n = w_ada.shape[1]
    tn = 512
    return pl.pallas_call(
        _mod_kernel,
        out_shape=jax.ShapeDtypeStruct((rows, n), F32),
        grid=(n // tn,),
        in_specs=[pl.BlockSpec((rows, d), lambda j: (0, 0)),
                  pl.BlockSpec((d, tn), lambda j: (0, j)),
                  pl.BlockSpec((1, tn), lambda j: (0, j))],
        out_specs=pl.BlockSpec((rows, tn), lambda j: (0, j)),
        compiler_params=_cparams(("arbitrary",)),
        name="adaln_modulation",
    )(cvec, w_ada, b_ada.reshape(1, n))


def _inproj_kernel(x_ref, mod_ref, g_ref, w_ref, o_ref, *, tn, segments):
    x = x_ref[0]
    y = x * lax.rsqrt(jnp.mean(x * x, axis=-1, keepdims=True) + RMS_EPS) * g_ref[...]
    m = mod_ref[0]
    h = (y * (1.0 + m[1:2]) + m[0:1]).astype(BF16)
    for src, dst, size in segments:
        for j0 in range(0, size, tn):
            w = min(tn, size - j0)
            o_ref[0, :, dst + j0:dst + j0 + w] = jnp.dot(h, w_ref[:, src + j0:src + j0 + w],
                                                         preferred_element_type=F32)


def _inproj(x, mod, norm_g, w_bf16, tm, segments):
    b, t, d = x.shape
    n = w_bf16.shape[1]
    return pl.pallas_call(
        functools.partial(_inproj_kernel, tn=2 * MXU_DIM, segments=segments),
        out_shape=jax.ShapeDtypeStruct((b, t, n), F32),
        grid=(b, t // tm),
        in_specs=[pl.BlockSpec((1, tm, d), lambda bi, i: (bi, i, 0)),
                  pl.BlockSpec((1, 8, d), lambda bi, i: (bi, 0, 0)),
                  pl.BlockSpec((1, d), lambda bi, i: (0, 0)),
                  pl.BlockSpec((d, n), lambda bi, i: (0, 0), pipeline_mode=pl.Buffered(1))],
        out_specs=pl.BlockSpec((1, tm, n), lambda bi, i: (bi, i, 0)),
        compiler_params=_cparams(("arbitrary", "arbitrary")),
        name="norm_mod_inproj",
    )(x, mod, norm_g.reshape(1, d), w_bf16)


_ROT = ((1.0, 0, -1.0, 1), (-1.0, 1, -1.0, 0), (-1.0, 0, 1.0, 1), (1.0, 1, 1.0, 0))


def _fourier_kernel(xf_ref, cc_ref, er_ref, ei_ref, o_ref, g_ref, il_ref, *, n1, t2, rows):
    width = xf_ref.shape[2]

    @pl.when(pl.program_id(1) == 0)
    def _():
        def body(i, carry):
            r0 = pl.multiple_of(i * rows, rows)
            pq = []
            for a in range(n1):
                x = xf_ref[0, pl.ds(a * t2 + r0, rows), :]
                parts = [_bdot(x[:, g * FOURIER_GROUP:(g + 1) * FOURIER_GROUP], cc_ref[...])
                         for g in range(width // FOURIER_GROUP)]
                pq.append((jnp.concatenate([p[:, :FOURIER_GROUP] for p in parts], axis=1),
                           jnp.concatenate([p[:, FOURIER_GROUP:] for p in parts], axis=1)))
            for f1 in range(n1):
                re = im = None
                for a in range(n1):
                    sr, cr, si, ci = _ROT[(a * f1) % 4]
                    tr, ti = sr * pq[a][cr], si * pq[a][ci]
                    re = tr if re is None else re + tr
                    im = ti if im is None else im + ti
                g_ref[f1, pl.ds(r0, rows), :] = re.astype(BF16)
                g_ref[f1, pl.ds(t2 + r0, rows), :] = im.astype(BF16)
            return carry
        lax.fori_loop(0, t2 // rows, body, 0)

    tf = er_ref.shape[1]
    for f1 in range(n1):
        res = (jnp.dot(er_ref[f1], g_ref[f1, 0:t2], preferred_element_type=F32)
               + jnp.dot(ei_ref[f1], g_ref[f1, t2:2 * t2], preferred_element_type=F32))
        if n1 == 1:
            o_ref[0] = res
        else:
            for g in range(width // LANES):
                il_ref[g, pl.ds(f1, tf, stride=n1), :] = res[:, g * LANES:(g + 1) * LANES]
    if n1 > 1:
        for g in range(width // LANES):
            o_ref[0, :, g * LANES:(g + 1) * LANES] = il_ref[g]


def _dft_consts(t, n1):
    t2 = t // n1
    g = FOURIER_GROUP
    j = np.arange(g)
    ang = 2.0 * np.pi * ((j[:, None] * j[None, :]) % g) / g
    cc = jnp.asarray(np.concatenate([np.cos(ang), np.sin(ang)], axis=1) / math.sqrt(g), dtype=F32).astype(BF16)
    scale = 1.0 / math.sqrt(t)
    if n1 == 1:
        k = np.arange(t)
        ang = 2.0 * np.pi * ((k[:, None] * k[None, :]) % t) / t
        tab = lambda m: jnp.asarray(m[None] * scale, dtype=F32).astype(BF16)
        return tab(np.cos(ang)), tab(np.sin(ang)), cc
    sa = 1 << (int(math.log2(t2)) // 2)
    sb = t2 // sa
    assert sa * sb == t2 and n1 * t2 == t
    tt = np.arange(t2)
    f1 = np.arange(n1)[:, None, None]
    bb = np.arange(sb)[None, :, None]
    ang_x = 2.0 * np.pi * ((tt[None, None, :] * (f1 + n1 * bb)) % t) / t
    aa = np.arange(sa)[:, None]
    ang_y = 2.0 * np.pi * ((tt[None, :] * aa) % sa) / sa
    xr, xi = jnp.asarray(np.cos(ang_x) * scale, F32), jnp.asarray(-np.sin(ang_x) * scale, F32)
    yr, yi = jnp.asarray(np.cos(ang_y), F32), jnp.asarray(-np.sin(ang_y), F32)
    xr, xi = xr[:, None], xi[:, None]
    yr, yi = yr[None, :, None], yi[None, :, None]
    re = (xr * yr - xi * yi).reshape(n1, t2, t2)
    im = (xr * yi + xi * yr).reshape(n1, t2, t2)
    return re.astype(BF16), (-im).astype(BF16), cc


def _fourier(u, t, width, n1, tf):
    b = u.shape[0]
    t2 = t // n1
    e_re, e_nim, cc = _dft_consts(t, n1)
    e_spec = pl.BlockSpec((n1, tf, t2), lambda bi, i: (0, i, 0))
    kern = functools.partial(_fourier_kernel, n1=n1, t2=t2, rows=min(t2, 256))
    return pl.pallas_call(
        kern,
        out_shape=jax.ShapeDtypeStruct((b, t, width), F32),
        grid=(b, t2 // tf),
        in_specs=[pl.BlockSpec((1, t, width), lambda bi, i: (bi, 0, 0)),
                  pl.BlockSpec((FOURIER_GROUP, 2 * FOURIER_GROUP), lambda bi, i: (0, 0)),
                  e_spec, e_spec],
        out_specs=pl.BlockSpec((1, n1 * tf, width), lambda bi, i: (bi, i, 0)),
        scratch_shapes=[pltpu.VMEM((n1, 2 * t2, width), BF16),
                        pltpu.VMEM((width // LANES, n1 * tf if n1 > 1 else 8, LANES), F32)],
        compiler_params=_cparams(("arbitrary", "arbitrary")),
        name="fourier_mix",
    )(u, cc, e_re, e_nim)


def _shift(x, prev_halo, next_halo, t_total, row0, grid_shift):
    rows, lanes = x.shape
    row = lax.broadcasted_iota(jnp.int32, (rows, lanes), 0)
    lane = lax.broadcasted_iota(jnp.int32, (rows, lanes), 1)
    before = pltpu.roll(x, 1, 0)
    after = pltpu.roll(x, rows - 1, 0)
    if not grid_shift:
        prev = jnp.where(row == 0, 0.0, before)
        nxt = jnp.where(row == rows - 1, 0.0, after)
        return jnp.where((lane & 1) == 0, prev, nxt)
    col = row & (GRID_W - 1)
    grow = row + row0
    left = jnp.where(col == 0, 0.0, before)
    right = jnp.where(col == GRID_W - 1, 0.0, after)
    up = jnp.concatenate([prev_halo, x[:rows - GRID_W]], axis=0)
    up = jnp.where(grow < GRID_W, 0.0, up)
    down = jnp.concatenate([x[GRID_W:], next_halo], axis=0)
    down = jnp.where(grow >= t_total - GRID_W, 0.0, down)
    m = lane & 3
    return jnp.where(m == 0, left, jnp.where(m == 1, right, jnp.where(m == 2, up, down)))


def _cumsum_matrices(rows):
    t = np.arange(rows)
    same = (t[:, None] // CHUNK) == (t[None, :] // CHUNK)
    tri = np.stack([same & (t[None, :] <= t[:, None]), same & (t[None, :] >= t[:, None])])
    return jnp.asarray(tri.astype(np.float32), dtype=BF16)


def _prep_kernel(*refs, grid_shift, t_total, width, cw):
    if grid_shift:
        (r_ref, k_ref, v_ref, d_ref, rp_ref, kp_ref, vp_ref, dp_ref, rn_ref, kn_ref, vn_ref, dn_ref,
         mu_ref, mud_ref, kk_ref, wcat_ref, bias_ref, seg_ref, tri_ref,
         ro_ref, ko_ref, vo_ref, kko_ref, lwf_ref, lwb_ref, af_ref, ab_ref) = refs
    else:
        (r_ref, k_ref, v_ref, d_ref, mu_ref, mud_ref, kk_ref, wcat_ref, bias_ref, seg_ref, tri_ref,
         ro_ref, ko_ref, vo_ref, kko_ref, lwf_ref, lwb_ref, af_ref, ab_ref) = refs
        rp_ref = kp_ref = vp_ref = dp_ref = rn_ref = kn_ref = vn_ref = dn_ref = None
    rows = r_ref.shape[1]
    row0 = pl.program_id(1) * rows

    def lerp(x_ref, p_ref, n_ref, mu, sl):
        x = x_ref[0, :, sl]
        ph = p_ref[0, :, sl] if grid_shift else None
        nh = n_ref[0, :, sl] if grid_shift else None
        s = _shift(x, ph, nh, t_total, row0, grid_shift)
        return x + mu * (s - x)

    dsl = slice(0, LANES)
    d = lerp(d_ref, dp_ref, dn_ref, mud_ref[...], dsl)
    lane = lax.broadcasted_iota(jnp.int32, d.shape, 1)
    d = jnp.where(lane < HEAD, jnp.tanh(d), d).astype(BF16)
    outs = (lwf_ref, lwb_ref, af_ref, ab_ref)
    for j in range(width // cw):
        sl = slice(j * cw, (j + 1) * cw)
        ro_ref[0, :, sl] = lerp(r_ref, rp_ref, rn_ref, mu_ref[0:1, sl], sl)
        vo_ref[0, :, sl] = lerp(v_ref, vp_ref, vn_ref, mu_ref[2:3, sl], sl)
        k = lerp(k_ref, kp_ref, kn_ref, mu_ref[1:2, sl], sl)
        ko_ref[0, :, sl] = k
        kk = k * kk_ref[0:1, sl]
        ss = _dot_exact_rhs(kk * kk, seg_ref[...])
        kko_ref[0, :, sl] = kk / jnp.maximum(jnp.sqrt(ss), 1e-12)
        for z in range(4):
            wsl = slice(z * width + j * cw, z * width + (j + 1) * cw)
            pre = jnp.dot(d, wcat_ref[:, wsl], preferred_element_type=F32) + bias_ref[z:z + 1, sl]
            sg = jax.nn.sigmoid(pre)
            if z < 2:
                hi, lo = _split2(sg * (-math.exp(-0.5)))
                outs[z][0, :, sl] = (jnp.dot(tri_ref[z], hi, preferred_element_type=F32)
                                     + jnp.dot(tri_ref[z], lo, preferred_element_type=F32))
            else:
                outs[z][0, :, sl] = sg


def _rwkv_prep(u, mu_shift, k_k, w_up, a_up, w0, a0, grid_shift, tt):
    b, t, _ = u.shape
    width = k_k.shape[0]
    cw = MXU_DIM
    mu = mu_shift[:3 * width].reshape(3, width)
    mud = mu_shift[3 * width:].reshape(1, LANES)
    zpad = jnp.zeros((HEAD, width), F32)
    wcat = jnp.concatenate([jnp.concatenate([w_up[0], zpad], 0), jnp.concatenate([w_up[1], zpad], 0),
                            jnp.concatenate([zpad, a_up[0]], 0), jnp.concatenate([zpad, a_up[1]], 0)],
                           axis=1).astype(BF16)
    bias = jnp.concatenate([w0, a0], axis=0)
    seg = _seg_matrix(cw)
    d_blk = (u.shape[2] - LANES) // LANES
    main = [pl.BlockSpec((1, tt, width), lambda bi, i, c=c: (bi, i, c)) for c in (1, 2, 3)]
    main.append(pl.BlockSpec((1, tt, LANES), lambda bi, i: (bi, i, d_blk)))
    ins = [u, u, u, u]
    specs = list(main)
    if grid_shift:
        hb = tt // GRID_W
        last = t // GRID_W - 1
        for off in (-1, hb):
            def imap(bi, i, c, off=off):
                return (bi, jnp.clip(i * hb + off, 0, last), c)
            specs += [pl.BlockSpec((1, GRID_W, width), functools.partial(imap, c=c)) for c in (1, 2, 3)]
            specs.append(pl.BlockSpec((1, GRID_W, LANES), functools.partial(imap, c=d_blk)))
            ins += [u, u, u, u]
    const = lambda shape: pl.BlockSpec(shape, lambda bi, i: (0, 0))
    specs += [const((3, width)), const((1, LANES)), const((1, width)), const((LANES, 4 * width)),
              const((4, width)), const((cw, cw)), pl.BlockSpec((2, tt, tt), lambda bi, i: (0, 0, 0))]
    ins += [mu, mud, k_k.reshape(1, width), wcat, bias, seg, _cumsum_matrices(tt)]
    out_spec = pl.BlockSpec((1, tt, width), lambda bi, i: (bi, i, 0))
    kern = functools.partial(_prep_kernel, grid_shift=grid_shift, t_total=t, width=width, cw=cw)
    return pl.pallas_call(
        kern,
        out_shape=[jax.ShapeDtypeStruct((b, t, width), F32)] * 8,
        grid=(b, t // tt),
        in_specs=specs,
        out_specs=[out_spec] * 8,
        compiler_params=_cparams(("arbitrary", "arbitrary")),
        name="rwkv_prep",
    )(*ins)


def _seg_matrix(n):
    i = np.arange(n) // HEAD
    return jnp.asarray((i[:, None] == i[None, :]).astype(np.float32), dtype=BF16)


def _heads_stacked(x, m0):
    return jnp.concatenate([jnp.where(m0, x, 0.0), jnp.where(m0, 0.0, x)], axis=0)


class _Handoff:
    value = None


def _scan_chain(in_refs, row0, ls, k_a, reverse, state_in, state_out, y_ref, delay):
    for _ in range(delay):
        yield
    r_ref, k_ref, v_ref, kk_ref, cl_ref, a_ref = in_refs
    n = TBLK
    rs = slice(row0, row0 + n)
    cl = cl_ref[0, rs, ls]
    pos = lax.broadcasted_iota(jnp.int32, (n, LANES), 0) & (CHUNK - 1)
    if reverse:
        cl_ex = jnp.where(pos == CHUNK - 1, 0.0, pltpu.roll(cl, n - 1, 0))
    else:
        cl_ex = jnp.where(pos == 0, 0.0, pltpu.roll(cl, 1, 0))
    n_chunks = n // CHUNK
    ends = [c * CHUNK if reverse else (c + 1) * CHUNK - 1 for c in range(n_chunks)]
    tot_rows = [cl[e:e + 1] for e in ends]
    tot = jnp.concatenate([jnp.broadcast_to(tr, (CHUNK, LANES)) for tr in tot_rows], axis=0)
    r, k, v, kk, a = r_ref[0, rs, ls], k_ref[0, rs, ls], v_ref[0, rs, ls], kk_ref[0, rs, ls], a_ref[0, rs, ls]
    kd = k * (1.0 + (a - 1.0) * k_a)
    b = kk * a
    at = -kk * jnp.exp(cl_ex)
    rt = r * jnp.exp(cl)
    e_neg = jnp.exp(-cl)
    e_tail = jnp.exp(tot - cl)
    bh = b * e_tail
    kh = kd * e_tail
    w_chunk = [jnp.exp(tr) for tr in tot_rows]

    m0 = lax.broadcasted_iota(jnp.int32, (n, LANES), 1) < HEAD
    lhs = jnp.concatenate([_heads_stacked(at, m0), _heads_stacked(rt, m0)], axis=0)
    rhs = jnp.concatenate([b * e_neg, kd * e_neg], axis=0)
    g = _bdot_nt(lhs, rhs)
    yield

    row = lax.broadcasted_iota(jnp.int32, (n, 2 * n), 0)
    col = lax.broadcasted_iota(jnp.int32, (n, 2 * n), 1) & (n - 1)
    same = (row // CHUNK) == (col // CHUNK)
    if reverse:
        strict, incl = same & (col > row), same & (col >= row)
    else:
        strict, incl = same & (col < row), same & (col <= row)
    pick = lambda r0, cs: jnp.concatenate([g[r0:r0 + n, cs], g[r0 + n:r0 + 2 * n, cs]], axis=1)
    left, right = slice(0, n), slice(n, 2 * n)
    a_ab = jnp.where(strict, pick(0, left), 0.0)
    a_kr = jnp.concatenate([jnp.where(strict, pick(0, right), 0.0),
                            jnp.where(incl, pick(2 * n, right), 0.0)], axis=0).astype(BF16)
    a_rb = jnp.where(incl, pick(2 * n, left), 0.0).astype(BF16)

    blocks = 2 * n_chunks
    brow = lax.broadcasted_iota(jnp.int32, (2 * n, 2 * n), 0)
    bcol = lax.broadcasted_iota(jnp.int32, (2 * n, 2 * n), 1)
    on_block = (brow // CHUNK) == (bcol // CHUNK)
    spread = lambda x: jnp.where(on_block, jnp.concatenate([x.astype(BF16)] * blocks, axis=0), 0.0)
    steps = int(math.log2(CHUNK)) - 1
    q = a_ab[:CHUNK]
    for c in range(1, n_chunks):
        q = q + a_ab[c * CHUNK:(c + 1) * CHUNK]
    p = _bdot(q, spread(q))
    kr = jnp.dot(a_kr, _heads_stacked(v, m0).astype(BF16), preferred_element_type=F32)
    yield
    akv, rkv = kr[:n], kr[n:]
    for _ in range(steps - 1):
        res = _bdot(jnp.concatenate([p, q], axis=0), spread(p))
        yield
        q = q + p + res[CHUNK:]
        p = res[:CHUNK]
    fin = _bdot(q, spread(p))
    yield
    q = q + p + fin
    q16 = jnp.where(same, jnp.concatenate([q] * n_chunks, axis=0), 0.0).astype(BF16)

    qx = jnp.dot(q16, jnp.concatenate([_heads_stacked(at, m0), _heads_stacked(akv, m0)], axis=1).astype(BF16),
                 preferred_element_type=F32)
    yield
    a_hat = at + qx[:, :LANES]
    uv = akv + qx[:, LANES:]
    both = jnp.dot(a_rb, jnp.concatenate([_heads_stacked(a_hat, m0), _heads_stacked(uv, m0)],
                                         axis=1).astype(BF16), preferred_element_type=F32)
    yield
    r_hat = rt + both[:, :LANES]
    yv = both[:, LANES:] + rkv

    srow = lax.broadcasted_iota(jnp.int32, (LANES, LANES), 0)
    scol = lax.broadcasted_iota(jnp.int32, (LANES, LANES), 1)
    pair_diag = (srow < HEAD) == (scol < HEAD)
    while state_in.value is None:
        yield
    state = state_in.value
    for c in (range(n_chunks - 1, -1, -1) if reverse else range(n_chunks)):
        sl = slice(c * CHUNK, (c + 1) * CHUNK)
        proj = _bdot_nt(jnp.concatenate([a_hat[sl], r_hat[sl]], axis=0), state)
        yield
        u_c = proj[:CHUNK] + uv[sl]
        y_ref[0, row0 + c * CHUNK:row0 + (c + 1) * CHUNK, ls] = proj[CHUNK:] + yv[sl]
        upd = _bdot_tn(jnp.concatenate([u_c, v[sl]], axis=0), jnp.concatenate([bh[sl], kh[sl]], axis=0))
        yield
        state = state * w_chunk[c] + jnp.where(pair_diag, upd, 0.0)
    state_out.value = state


def _run_interleaved(chains):
    chains = list(chains)
    while chains:
        alive = []
        for ch in chains:
            try:
                next(ch)
                alive.append(ch)
            except StopIteration:
                pass
        chains = alive


def _scan_kernel(*refs, has_init, want_state, groups):
    fwd_refs, bwd_refs = refs[:6], refs[6:12]
    ka_ref = refs[12]
    pos = 13
    s0_ref = None
    if has_init:
        s0_ref = refs[pos]
        pos += 1
    yf_ref, yb_ref = refs[pos:pos + 2]
    pos += 2
    so_ref = None
    if want_state:
        so_ref = refs[pos]
        pos += 1
    st_ref = refs[pos]
    i = pl.program_id(2)

    @pl.when(i == 0)
    def _():
        if has_init:
            st_ref[...] = s0_ref[0]
        else:
            st_ref[...] = jnp.zeros(st_ref.shape, F32)

    blocks = yf_ref.shape[1] // TBLK
    chains, finals = [], []
    for g in range(groups):
        ls = slice(g * LANES, (g + 1) * LANES)
        k_a = ka_ref[:, ls]
        for z, (refs_z, y_ref) in enumerate(((fwd_refs, yf_ref), (bwd_refs, yb_ref))):
            carry = _Handoff()
            carry.value = st_ref[z, g]
            for j in range(blocks):
                row0 = (blocks - 1 - j if z else j) * TBLK
                nxt = _Handoff()
                chains.append(_scan_chain(refs_z, row0, ls, k_a, bool(z), carry, nxt, y_ref,
                                          j * STATE_FREE_ROUNDS))
                carry = nxt
            finals.append(((z, g), carry))
    _run_interleaved(chains)
    for idx, cell in finals:
        st_ref[idx] = cell.value

    if want_state:
        @pl.when(i == pl.num_programs(2) - 1)
        def _():
            for z in range(2):
                for g in range(groups):
                    s = st_ref[z, g]
                    so_ref[0, z, 2 * g] = s[:HEAD, :HEAD]
                    so_ref[0, z, 2 * g + 1] = s[HEAD:, HEAD:]


def _rwkv_scan(r, k, v, kk, cl_f, cl_b, a_f, a_b, k_a, s0, want_state):
    b, t, width = r.shape
    pairs = width // LANES
    g = SCAN_PAIRS
    gw = g * LANES
    rows = SCAN_BLOCKS * TBLK
    nt = t // rows
    fwd = pl.BlockSpec((1, rows, gw), lambda bi, p, i: (bi, i, p))
    bwd = pl.BlockSpec((1, rows, gw), lambda bi, p, i: (bi, nt - 1 - i, p))
    st_spec = pl.BlockSpec((1, 2, g, LANES, LANES), lambda bi, p, i: (bi, 0, p, 0, 0))
    specs = [fwd] * 6 + [bwd] * 6 + [pl.BlockSpec((1, gw), lambda bi, p, i: (0, p))]
    ins = [r, k, v, kk, cl_f, a_f, r, k, v, kk, cl_b, a_b, k_a.reshape(1, width)]
    if s0 is not None:
        specs.append(st_spec)
        ins.append(s0)
    y_shape = jax.ShapeDtypeStruct((b, t, width), F32)
    out_shape = [y_shape, y_shape]
    out_specs = [fwd, bwd]
    if want_state:
        out_shape.append(jax.ShapeDtypeStruct((b, 2, 2 * pairs, HEAD, HEAD), F32))
        out_specs.append(pl.BlockSpec((1, 2, 2 * g, HEAD, HEAD), lambda bi, p, i: (bi, 0, p, 0, 0)))
    kern = functools.partial(_scan_kernel, has_init=s0 is not None, want_state=want_state, groups=g)
    return pl.pallas_call(
        kern,
        out_shape=out_shape,
        grid=(b, pairs // g, nt),
        in_specs=specs,
        out_specs=out_specs,
        scratch_shapes=[pltpu.VMEM((2, g, LANES, LANES), F32)],
        compiler_params=_cparams(("arbitrary", "arbitrary", "arbitrary")),
        name="rwkv7_scan",
    )(*ins)


def _pair_states(s):
    b, z, h, n, _ = s.shape
    sp = s.reshape(b, z, h // 2, 2, n, n)
    zero = jnp.zeros_like(sp[:, :, :, 0])
    top = jnp.concatenate([sp[:, :, :, 0], zero], axis=-1)
    bot = jnp.concatenate([zero, sp[:, :, :, 1]], axis=-1)
    return jnp.concatenate([top, bot], axis=-2)


def _post_kernel(yf_ref, yb_ref, r_ref, k_ref, v_ref, gf_ref, gr_ref, g0_ref, g1_ref, fo_ref, x_ref, mod_ref,
                 vec_ref, seg_ref, wf_ref, wr_ref, wo_ref, o_ref, z_ref, *, cw):
    width = yf_ref.shape[2]
    inv_n = 1.0 / HEAD
    for j in range(width // cw):
        sl = slice(j * cw, (j + 1) * cw)
        seg = seg_ref[...]
        y = yf_ref[0, :, sl] + yb_ref[0, :, sl]
        mean = _dot_exact_rhs(y, seg) * inv_n
        d = y - mean
        var = _dot_exact_rhs(d * d, seg) * inv_n
        yn = d * lax.rsqrt(var + GN_EPS) * vec_ref[0:1, sl] + vec_ref[1:2, sl]
        rk = _dot_exact_rhs(r_ref[0, :, sl] * k_ref[0, :, sl] * vec_ref[2:3, sl], seg)
        z = (yn + rk * v_ref[0, :, sl]) * _silu(gr_ref[0, :, sl])
        z_ref[:, sl] = z.astype(BF16)
    out_r = jnp.dot(z_ref[...], wr_ref[...], preferred_element_type=F32)
    zf = (fo_ref[0] * _silu(gf_ref[0])).astype(BF16)
    out_f = jnp.dot(zf, wf_ref[...], preferred_element_type=F32)
    merged = jax.nn.sigmoid(g0_ref[0]) * out_f + jax.nn.sigmoid(g1_ref[0]) * out_r
    o = jnp.dot(merged.astype(BF16), wo_ref[...], preferred_element_type=F32)
    xn = x_ref[0] + mod_ref[0][2:3] * o
    o_ref[0] = xn * lax.rsqrt(jnp.mean(xn * xn, axis=-1, keepdims=True) + RMS_EPS) * vec_ref[3:4, :]


def _post(yf, yb, r, k, v, u, fo, x, mod, lnx_g, lnx_b, r_k, final_g, w_proj_f, w_proj_r, w_out, tm):
    b, t, d = x.shape
    width = yf.shape[2]
    fw = fo.shape[2]
    cw = MXU_DIM
    vec = jnp.stack([lnx_g, lnx_b, r_k.reshape(-1), final_g], axis=0)
    tok = lambda w, c: pl.BlockSpec((1, tm, w), lambda bi, i: (bi, i, c))
    const = lambda shape: pl.BlockSpec(shape, lambda bi, i: (0,) * len(shape))
    gr_off = 2 * fw + 3 * width
    mg_off = gr_off + width
    assert gr_off % width == 0 and mg_off % d == 0
    gr_blk, mg_blk = gr_off // width, mg_off // d
    specs = [tok(width, 0)] * 5 + [tok(fw, 1), tok(width, gr_blk), tok(d, mg_blk), tok(d, mg_blk + 1),
                                   tok(fw, 0), tok(d, 0), pl.BlockSpec((1, 8, d), lambda bi, i: (bi, 0, 0)),
                                   const((4, d)), const((cw, cw)), const((fw, d)), const((width, d)),
                                   const((d, d))]
    return pl.pallas_call(
        functools.partial(_post_kernel, cw=cw),
        out_shape=jax.ShapeDtypeStruct((b, t, d), F32),
        grid=(b, t // tm),
        in_specs=specs,
        out_specs=tok(d, 0),
        scratch_shapes=[pltpu.VMEM((tm, width), BF16)],
        compiler_params=_cparams(("arbitrary", "arbitrary")),
        name="post_mix",
    )(yf, yb, r, k, v, u, u, u, u, fo, x, mod, vec, _seg_matrix(cw),
      w_proj_f.astype(BF16), w_proj_r.astype(BF16), w_out.astype(BF16))


def _mixer_path(x, mod, s0, want_state, grid_shift, p, tiles):
    b, t, d = x.shape
    u = _inproj(x, mod, p["norm_g"], p["w_in"], tiles["inproj"], p["w_in_segments"])
    fo = _fourier(u, t, p["fourier_width"], tiles["fourier_n1"], tiles["fourier"])
    r, k, v, kk, cl_f, cl_b, a_f, a_b = _rwkv_prep(
        u, p["mu_shift"], p["k_k"], p["w_up"], p["a_up"], p["w0"], p["a0"], grid_shift, tiles["prep"])
    res = _rwkv_scan(r, k, v, kk, cl_f, cl_b, a_f, a_b, p["k_a"], s0, want_state)
    y = _post(res[0], res[1], r, k, v, u, fo, x, mod, p["lnx_g"], p["lnx_b"], p["r_k"], p["final_g"],
              p["w_proj_f"], p["w_proj_r"], p["w_out"], tiles["post"])
    return y, (res[2] if want_state else None)


def _tiles(t, grid_shift):
    if not grid_shift:
        return dict(inproj=t, fourier=t, fourier_n1=1, prep=t, post=t)
    return dict(inproj=256, fourier=256, fourier_n1=4, prep=256, post=256)


def kernel(x_prompt, x_sample, state_rwkv, c, c_ctx, norm_g, w_ada, b_ada, w_in, mu_shift, w0, w_up, a0,
           a_up, k_k, k_a, r_k, lnx_g, lnx_b, w_proj_f, w_proj_r, w_out, final_g):
    depth = w_in.shape[0]
    assert depth == 1, "the final norm is fused into the single layer's post kernel"
    bp, tp, d = x_prompt.shape
    bs, ts, _ = x_sample.shape
    width = k_k.shape[1]
    fw = w_proj_f.shape[1]
    l = 0
    sh_end = 2 * fw + 3 * width
    rank2 = mu_shift.shape[1] - 3 * width
    n_in = w_in.shape[2]
    segments = ((0, 0, sh_end), (sh_end + rank2, sh_end, n_in - sh_end - rank2), (sh_end, n_in - rank2, rank2))
    p = dict(norm_g=norm_g[l], w_in=w_in[l].astype(BF16), w_in_segments=segments, fourier_width=fw,
             mu_shift=mu_shift[l], k_k=k_k[l], w_up=w_up[l], a_up=a_up[l], w0=w0[l], a0=a0[l], k_a=k_a[l],
             lnx_g=lnx_g[l], lnx_b=lnx_b[l], r_k=r_k[l], final_g=final_g, w_proj_f=w_proj_f[l],
             w_proj_r=w_proj_r[l], w_out=w_out[l])

    rows = 8
    cvec = jnp.concatenate([c_ctx[None], c, jnp.zeros((rows - 1 - bs, d), F32)], axis=0)
    m = _modulation(cvec, w_ada[l], b_ada[l])
    m3 = m.reshape(rows, 3, d)
    mod = jnp.concatenate([m3, jnp.zeros((rows, 5, d), F32)], axis=1)

    yp, sp = _mixer_path(x_prompt, jnp.broadcast_to(mod[0:1], (bp, 8, d)), None, True, False, p,
                         _tiles(tp, False))
    s0 = _pair_states(state_rwkv[:, l])
    ys, _ = _mixer_path(x_sample, mod[1:1 + bs], s0, False, True, p, _tiles(ts, True))
    new_state = sp[:, None]
    return yp, ys, new_state
```

```python
import functools
import math

import numpy as np
import jax
import jax.numpy as jnp
from jax import lax
from jax.experimental import pallas as pl
from jax.experimental.pallas import tpu as pltpu

F32 = jnp.float32
BF16 = jnp.bfloat16

LANES = 128
MXU_DIM = 256
VMEM_LIMIT = 56 * 1024 * 1024

HEAD = 64
GRID_W = 64
FOURIER_GROUP = 128
RMS_EPS = 1e-6
GN_EPS = 64e-5

TBLK = 128
CHUNK = 64
SCAN_PAIRS = 8
SCAN_BLOCKS = 1
STATE_FREE_ROUNDS = int(math.log2(CHUNK)) + 3


def _cparams(sem):
    return pltpu.CompilerParams(dimension_semantics=sem, vmem_limit_bytes=VMEM_LIMIT)


def _bdot(a, b):
    return jnp.dot(a.astype(BF16), b.astype(BF16), preferred_element_type=F32)


def _bdot_nt(a, b):
    return lax.dot_general(a.astype(BF16), b.astype(BF16), (((1,), (1,)), ((), ())),
                           preferred_element_type=F32)


def _bdot_tn(a, b):
    return lax.dot_general(a.astype(BF16), b.astype(BF16), (((0,), (0,)), ((), ())),
                           preferred_element_type=F32)


def _split2(x):
    hi = x.astype(BF16)
    lo = (x - hi.astype(F32)).astype(BF16)
    return hi, lo


def _dot_exact_rhs(x, m):
    hi, lo = _split2(x)
    return (jnp.dot(hi, m, preferred_element_type=F32) + jnp.dot(lo, m, preferred_element_type=F32))


def _silu(x):
    return x * jax.nn.sigmoid(x)


def _mod_kernel(c_ref, w_ref, b_ref, o_ref):
    c = c_ref[...]
    o_ref[...] = jnp.dot(_silu(c), w_ref[...], precision=lax.Precision.HIGHEST,
                         preferred_element_type=F32) + b_ref[...]


def _modulation(cvec, w_ada, b_ada):
    rows, d = cvec.shape
    n = w_ada.shape[1]
    tn = n // 2
    return pl.pallas_call(
        _mod_kernel,
        out_shape=jax.ShapeDtypeStruct((rows, n), F32),
        grid=(n // tn,),
        in_specs=[pl.BlockSpec((rows, d), lambda j: (0, 0)),
                  pl.BlockSpec((d, tn), lambda j: (0, j)),
                  pl.BlockSpec((1, tn), lambda j: (0, j))],
        out_specs=pl.BlockSpec((rows, tn), lambda j: (0, j)),
        compiler_params=_cparams(("arbitrary",)),
        name="adaln_modulation",
    )(cvec, w_ada, b_ada.reshape(1, n))


def _inproj_kernel(x_ref, mod_ref, g_ref, w_ref, o_ref, *, tn, segments):
    x = x_ref[0]
    y = x * lax.rsqrt(jnp.mean(x * x, axis=-1, keepdims=True) + RMS_EPS) * g_ref[...]
    m = mod_ref[0]
    h = (y * (1.0 + m[1:2]) + m[0:1]).astype(BF16)
    for src, dst, size in segments:
        for j0 in range(0, size, tn):
            w = min(tn, size - j0)
            o_ref[0, :, dst + j0:dst + j0 + w] = jnp.dot(h, w_ref[:, src + j0:src + j0 + w],
                                                         preferred_element_type=F32)


def _inproj(x, mod, norm_g, w_bf16, tm, segments):
    b, t, d = x.shape
    n = w_bf16.shape[1]
    return pl.pallas_call(
        functools.partial(_inproj_kernel, tn=2 * MXU_DIM, segments=segments),
        out_shape=jax.ShapeDtypeStruct((b, t, n), F32),
        grid=(b, t // tm),
        in_specs=[pl.BlockSpec((1, tm, d), lambda bi, i: (bi, i, 0)),
                  pl.BlockSpec((1, 8, d), lambda bi, i: (bi, 0, 0)),
                  pl.BlockSpec((1, d), lambda bi, i: (0, 0)),
                  pl.BlockSpec((d, n), lambda bi, i: (0, 0), pipeline_mode=pl.Buffered(1))],
        out_specs=pl.BlockSpec((1, tm, n), lambda bi, i: (bi, i, 0)),
        compiler_params=_cparams(("arbitrary", "arbitrary")),
        name="norm_mod_inproj",
    )(x, mod, norm_g.reshape(1, d), w_bf16)


_ROT = ((1.0, 0, -1.0, 1), (-1.0, 1, -1.0, 0), (-1.0, 0, 1.0, 1), (1.0, 1, 1.0, 0))


def _fourier_kernel(xf_ref, cc_ref, er_ref, ei_ref, o_ref, g_ref, il_ref, *, n1, t2, rows):
    width = xf_ref.shape[2]

    @pl.when(pl.program_id(1) == 0)
    def _():
        def body(i, carry):
            r0 = pl.multiple_of(i * rows, rows)
            pq = []
            for a in range(n1):
                x = xf_ref[0, pl.ds(a * t2 + r0, rows), :]
                parts = [_bdot(x[:, g * FOURIER_GROUP:(g + 1) * FOURIER_GROUP], cc_ref[...])
                         for g in range(width // FOURIER_GROUP)]
                pq.append((jnp.concatenate([p[:, :FOURIER_GROUP] for p in parts], axis=1),
                           jnp.concatenate([p[:, FOURIER_GROUP:] for p in parts], axis=1)))
            for f1 in range(n1):
                re = im = None
                for a in range(n1):
                    sr, cr, si, ci = _ROT[(a * f1) % 4]
                    tr, ti = sr * pq[a][cr], si * pq[a][ci]
                    re = tr if re is None else re + tr
                    im = ti if im is None else im + ti
                g_ref[f1, pl.ds(r0, rows), :] = re.astype(BF16)
                g_ref[f1, pl.ds(t2 + r0, rows), :] = im.astype(BF16)
            return carry
        lax.fori_loop(0, t2 // rows, body, 0)

    tf = er_ref.shape[1]
    for f1 in range(n1):
        res = (jnp.dot(er_ref[f1], g_ref[f1, 0:t2], preferred_element_type=F32)
               + jnp.dot(ei_ref[f1], g_ref[f1, t2:2 * t2], preferred_element_type=F32))
        if n1 == 1:
            o_ref[0] = res
        else:
            for g in range(width // LANES):
                il_ref[g, pl.ds(f1, tf, stride=n1), :] = res[:, g * LANES:(g + 1) * LANES]
    if n1 > 1:
        for g in range(width // LANES):
            o_ref[0, :, g * LANES:(g + 1) * LANES] = il_ref[g]


def _dft_consts(t, n1):
    t2 = t // n1
    g = FOURIER_GROUP
    j = np.arange(g)
    ang = 2.0 * np.pi * ((j[:, None] * j[None, :]) % g) / g
    cc = jnp.asarray(np.concatenate([np.cos(ang), np.sin(ang)], axis=1) / math.sqrt(g), dtype=F32).astype(BF16)
    scale = 1.0 / math.sqrt(t)
    if n1 == 1:
        k = np.arange(t)
        ang = 2.0 * np.pi * ((k[:, None] * k[None, :]) % t) / t
        tab = lambda m: jnp.asarray(m[None] * scale, dtype=F32).astype(BF16)
        return tab(np.cos(ang)), tab(np.sin(ang)), cc
    sa = 1 << (int(math.log2(t2)) // 2)
    sb = t2 // sa
    assert sa * sb == t2 and n1 * t2 == t
    tt = np.arange(t2)
    f1 = np.arange(n1)[:, None, None]
    bb = np.arange(sb)[None, :, None]
    ang_x = 2.0 * np.pi * ((tt[None, None, :] * (f1 + n1 * bb)) % t) / t
    aa = np.arange(sa)[:, None]
    ang_y = 2.0 * np.pi * ((tt[None, :] * aa) % sa) / sa
    xr, xi = jnp.asarray(np.cos(ang_x) * scale, F32), jnp.asarray(-np.sin(ang_x) * scale, F32)
    yr, yi = jnp.asarray(np.cos(ang_y), F32), jnp.asarray(-np.sin(ang_y), F32)
    xr, xi = xr[:, None], xi[:, None]
    yr, yi = yr[None, :, None], yi[None, :, None]
    re = (xr * yr - xi * yi).reshape(n1, t2, t2)
    im = (xr * yi + xi * yr).reshape(n1, t2, t2)
    return re.astype(BF16), (-im).astype(BF16), cc


def _fourier(u, t, width, n1, tf):
    b = u.shape[0]
    t2 = t // n1
    e_re, e_nim, cc = _dft_consts(t, n1)
    e_spec = pl.BlockSpec((n1, tf, t2), lambda bi, i: (0, i, 0))
    kern = functools.partial(_fourier_kernel, n1=n1, t2=t2, rows=min(t2, 256))
    return pl.pallas_call(
        kern,
        out_shape=jax.ShapeDtypeStruct((b, t, width), F32),
        grid=(b, t2 // tf),
        in_specs=[pl.BlockSpec((1, t, width), lambda bi, i: (bi, 0, 0)),
                  pl.BlockSpec((FOURIER_GROUP, 2 * FOURIER_GROUP), lambda bi, i: (0, 0)),
                  e_spec, e_spec],
        out_specs=pl.BlockSpec((1, n1 * tf, width), lambda bi, i: (bi, i, 0)),
        scratch_shapes=[pltpu.VMEM((n1, 2 * t2, width), BF16),
                        pltpu.VMEM((width // LANES, n1 * tf if n1 > 1 else 8, LANES), F32)],
        compiler_params=_cparams(("arbitrary", "arbitrary")),
        name="fourier_mix",
    )(u, cc, e_re, e_nim)


def _shift(x, prev_halo, next_halo, t_total, row0, grid_shift):
    rows, lanes = x.shape
    row = lax.broadcasted_iota(jnp.int32, (rows, lanes), 0)
    lane = lax.broadcasted_iota(jnp.int32, (rows, lanes), 1)
    before = pltpu.roll(x, 1, 0)
    after = pltpu.roll(x, rows - 1, 0)
    if not grid_shift:
        even = (lane & 1) == 0
        outside = (even & (row == 0)) | (~even & (row == rows - 1))
        return jnp.where(outside, 0.0, jnp.where(even, before, after))
    col = row & (GRID_W - 1)
    grow = row + row0
    up = jnp.concatenate([prev_halo, x[:rows - GRID_W]], axis=0)
    down = jnp.concatenate([x[GRID_W:], next_halo], axis=0)
    m = lane & 3
    outside = (((m == 0) & (col == 0)) | ((m == 1) & (col == GRID_W - 1))
               | ((m == 2) & (grow < GRID_W)) | ((m == 3) & (grow >= t_total - GRID_W)))
    pick = jnp.where(m == 0, before, jnp.where(m == 1, after, jnp.where(m == 2, up, down)))
    return jnp.where(outside, 0.0, pick)


def _cumsum_matrices(rows):
    t = np.arange(rows)
    same = (t[:, None] // CHUNK) == (t[None, :] // CHUNK)
    tri = np.stack([same & (t[None, :] <= t[:, None]), same & (t[None, :] >= t[:, None])])
    return jnp.asarray(tri.astype(np.float32), dtype=BF16)


def _prep_kernel(*refs, grid_shift, t_total, width, cw):
    if grid_shift:
        (r_ref, k_ref, v_ref, d_ref, rp_ref, kp_ref, vp_ref, dp_ref, rn_ref, kn_ref, vn_ref, dn_ref,
         mu_ref, mud_ref, kk_ref, wcat_ref, bias_ref, seg_ref, tri_ref,
         ro_ref, ko_ref, vo_ref, kko_ref, lwf_ref, lwb_ref, af_ref, ab_ref) = refs
    else:
        (r_ref, k_ref, v_ref, d_ref, mu_ref, mud_ref, kk_ref, wcat_ref, bias_ref, seg_ref, tri_ref,
         ro_ref, ko_ref, vo_ref, kko_ref, lwf_ref, lwb_ref, af_ref, ab_ref) = refs
        rp_ref = kp_ref = vp_ref = dp_ref = rn_ref = kn_ref = vn_ref = dn_ref = None
    rows = r_ref.shape[1]
    row0 = pl.program_id(1) * rows

    def lerp(x_ref, p_ref, n_ref, mu, sl):
        x = x_ref[0, :, sl]
        ph = p_ref[0, :, sl] if grid_shift else None
        nh = n_ref[0, :, sl] if grid_shift else None
        s = _shift(x, ph, nh, t_total, row0, grid_shift)
        return x + mu * (s - x)

    dsl = slice(0, LANES)
    d = lerp(d_ref, dp_ref, dn_ref, mud_ref[...], dsl)
    lane = lax.broadcasted_iota(jnp.int32, d.shape, 1)
    d = jnp.where(lane < HEAD, jnp.tanh(d), d).astype(BF16)
    outs = (lwf_ref, lwb_ref, af_ref, ab_ref)
    for j in range(width // cw):
        sl = slice(j * cw, (j + 1) * cw)
        ro_ref[0, :, sl] = lerp(r_ref, rp_ref, rn_ref, mu_ref[0:1, sl], sl)
        vo_ref[0, :, sl] = lerp(v_ref, vp_ref, vn_ref, mu_ref[2:3, sl], sl)
        k = lerp(k_ref, kp_ref, kn_ref, mu_ref[1:2, sl], sl)
        ko_ref[0, :, sl] = k
        kk = k * kk_ref[0:1, sl]
        ss = _dot_exact_rhs(kk * kk, seg_ref[...])
        kko_ref[0, :, sl] = kk / jnp.maximum(jnp.sqrt(ss), 1e-12)
        for z in range(4):
            wsl = slice(z * width + j * cw, z * width + (j + 1) * cw)
            pre = jnp.dot(d, wcat_ref[:, wsl], preferred_element_type=F32) + bias_ref[z:z + 1, sl]
            sg = jax.nn.sigmoid(pre)
            if z < 2:
                hi, lo = _split2(sg * (-math.exp(-0.5)))
                outs[z][0, :, sl] = (jnp.dot(tri_ref[z], hi, preferred_element_type=F32)
                                     + jnp.dot(tri_ref[z], lo, preferred_element_type=F32))
            else:
                outs[z][0, :, sl] = sg


def _rwkv_prep(u, mu_shift, k_k, w_up, a_up, w0, a0, grid_shift, tt):
    b, t, _ = u.shape
    width = k_k.shape[0]
    cw = MXU_DIM
    mu = mu_shift[:3 * width].reshape(3, width)
    mud = mu_shift[3 * width:].reshape(1, LANES)
    zpad = jnp.zeros((HEAD, width), F32)
    wcat = jnp.concatenate([jnp.concatenate([w_up[0], zpad], 0), jnp.concatenate([w_up[1], zpad], 0),
                            jnp.concatenate([zpad, a_up[0]], 0), jnp.concatenate([zpad, a_up[1]], 0)],
                           axis=1).astype(BF16)
    bias = jnp.concatenate([w0, a0], axis=0)
    seg = _seg_matrix(cw)
    d_blk = (u.shape[2] - LANES) // LANES
    main = [pl.BlockSpec((1, tt, width), lambda bi, i, c=c: (bi, i, c)) for c in (1, 2, 3)]
    main.append(pl.BlockSpec((1, tt, LANES), lambda bi, i: (bi, i, d_blk)))
    ins = [u, u, u, u]
    specs = list(main)
    if grid_shift:
        hb = tt // GRID_W
        last = t // GRID_W - 1
        for off in (-1, hb):
            def imap(bi, i, c, off=off):
                return (bi, jnp.clip(i * hb + off, 0, last), c)
            specs += [pl.BlockSpec((1, GRID_W, width), functools.partial(imap, c=c)) for c in (1, 2, 3)]
            specs.append(pl.BlockSpec((1, GRID_W, LANES), functools.partial(imap, c=d_blk)))
            ins += [u, u, u, u]
    const = lambda shape: pl.BlockSpec(shape, lambda bi, i: (0, 0))
    specs += [const((3, width)), const((1, LANES)), const((1, width)), const((LANES, 4 * width)),
              const((4, width)), const((cw, cw)), pl.BlockSpec((2, tt, tt), lambda bi, i: (0, 0, 0))]
    ins += [mu, mud, k_k.reshape(1, width), wcat, bias, seg, _cumsum_matrices(tt)]
    out_spec = pl.BlockSpec((1, tt, width), lambda bi, i: (bi, i, 0))
    kern = functools.partial(_prep_kernel, grid_shift=grid_shift, t_total=t, width=width, cw=cw)
    return pl.pallas_call(
        kern,
        out_shape=[jax.ShapeDtypeStruct((b, t, width), F32)] * 8,
        grid=(b, t // tt),
        in_specs=specs,
        out_specs=[out_spec] * 8,
        compiler_params=_cparams(("arbitrary", "arbitrary")),
        name="rwkv_prep",
    )(*ins)


def _seg_matrix(n):
    i = np.arange(n) // HEAD
    return jnp.asarray((i[:, None] == i[None, :]).astype(np.float32), dtype=BF16)


def _heads_stacked(x, m0):
    return jnp.concatenate([jnp.where(m0, x, 0.0), jnp.where(m0, 0.0, x)], axis=0)


class _Handoff:
    value = None


def _scan_chain(in_refs, row0, ls, k_a, reverse, state_in, state_out, y_ref, delay):
    for _ in range(delay):
        yield
    r_ref, k_ref, v_ref, kk_ref, cl_ref, a_ref = in_refs
    n = TBLK
    rs = slice(row0, row0 + n)
    cl = cl_ref[0, rs, ls]
    pos = lax.broadcasted_iota(jnp.int32, (n, LANES), 0) & (CHUNK - 1)
    if reverse:
        cl_ex = jnp.where(pos == CHUNK - 1, 0.0, pltpu.roll(cl, n - 1, 0))
    else:
        cl_ex = jnp.where(pos == 0, 0.0, pltpu.roll(cl, 1, 0))
    n_chunks = n // CHUNK
    ends = [c * CHUNK if reverse else (c + 1) * CHUNK - 1 for c in range(n_chunks)]
    tot_rows = [cl[e:e + 1] for e in ends]
    tot = jnp.concatenate([jnp.broadcast_to(tr, (CHUNK, LANES)) for tr in tot_rows], axis=0)
    r, k, v, kk, a = r_ref[0, rs, ls], k_ref[0, rs, ls], v_ref[0, rs, ls], kk_ref[0, rs, ls], a_ref[0, rs, ls]
    kd = k * (1.0 + (a - 1.0) * k_a)
    b = kk * a
    at = -kk * jnp.exp(cl_ex)
    rt = r * jnp.exp(cl)
    e_neg = jnp.exp(-cl)
    e_tail = jnp.exp(tot - cl)
    bh = b * e_tail
    kh = kd * e_tail
    w_chunk = [jnp.exp(tr) for tr in tot_rows]

    m0 = lax.broadcasted_iota(jnp.int32, (n, LANES), 1) < HEAD
    lhs = jnp.concatenate([_heads_stacked(at, m0), _heads_stacked(rt, m0)], axis=0)
    rhs = jnp.concatenate([b * e_neg, kd * e_neg], axis=0)
    g = _bdot_nt(lhs, rhs)
    yield

    row = lax.broadcasted_iota(jnp.int32, (n, 2 * n), 0)
    col = lax.broadcasted_iota(jnp.int32, (n, 2 * n), 1) & (n - 1)
    same = (row // CHUNK) == (col // CHUNK)
    if reverse:
        strict, incl = same & (col > row), same & (col >= row)
    else:
        strict, incl = same & (col < row), same & (col <= row)
    pick = lambda r0, cs: jnp.concatenate([g[r0:r0 + n, cs], g[r0 + n:r0 + 2 * n, cs]], axis=1)
    left, right = slice(0, n), slice(n, 2 * n)
    a_ab = jnp.where(strict, pick(0, left), 0.0)
    a_kr = jnp.concatenate([jnp.where(strict, pick(0, right), 0.0),
                            jnp.where(incl, pick(2 * n, right), 0.0)], axis=0).astype(BF16)
    a_rb = jnp.where(incl, pick(2 * n, left), 0.0).astype(BF16)

    blocks = 2 * n_chunks
    brow = lax.broadcasted_iota(jnp.int32, (2 * n, 2 * n), 0)
    bcol = lax.broadcasted_iota(jnp.int32, (2 * n, 2 * n), 1)
    on_block = (brow // CHUNK) == (bcol // CHUNK)
    spread = lambda x: jnp.where(on_block, jnp.concatenate([x.astype(BF16)] * blocks, axis=0), 0.0)
    steps = int(math.log2(CHUNK)) - 1
    q = a_ab[:CHUNK]
    for c in range(1, n_chunks):
        q = q + a_ab[c * CHUNK:(c + 1) * CHUNK]
    p = _bdot(q, spread(q))
    kr = jnp.dot(a_kr, _heads_stacked(v, m0).astype(BF16), preferred_element_type=F32)
    yield
    akv, rkv = kr[:n], kr[n:]
    for _ in range(steps - 1):
        res = _bdot(jnp.concatenate([p, q], axis=0), spread(p))
        yield
        q = q + p + res[CHUNK:]
        p = res[:CHUNK]
    fin = _bdot(q, spread(p))
    yield
    q = q + p + fin
    q16 = jnp.where(same, jnp.concatenate([q] * n_chunks, axis=0), 0.0).astype(BF16)

    qx = jnp.dot(q16, jnp.concatenate([_heads_stacked(at, m0), _heads_stacked(akv, m0)], axis=1).astype(BF16),
                 preferred_element_type=F32)
    yield
    a_hat = at + qx[:, :LANES]
    uv = akv + qx[:, LANES:]
    both = jnp.dot(a_rb, jnp.concatenate([_heads_stacked(a_hat, m0), _heads_stacked(uv, m0)],
                                         axis=1).astype(BF16), preferred_element_type=F32)
    yield
    r_hat = rt + both[:, :LANES]
    yv = both[:, LANES:] + rkv

    srow = lax.broadcasted_iota(jnp.int32, (LANES, LANES), 0)
    scol = lax.broadcasted_iota(jnp.int32, (LANES, LANES), 1)
    pair_diag = (srow < HEAD) == (scol < HEAD)
    while state_in.value is None:
        yield
    state = state_in.value
    for c in (range(n_chunks - 1, -1, -1) if reverse else range(n_chunks)):
        sl = slice(c * CHUNK, (c + 1) * CHUNK)
        proj = _bdot_nt(jnp.concatenate([a_hat[sl], r_hat[sl]], axis=0), state)
        yield
        u_c = proj[:CHUNK] + uv[sl]
        y_ref[0, row0 + c * CHUNK:row0 + (c + 1) * CHUNK, ls] = proj[CHUNK:] + yv[sl]
        upd = _bdot_tn(jnp.concatenate([u_c, v[sl]], axis=0), jnp.concatenate([bh[sl], kh[sl]], axis=0))
        yield
        state = state * w_chunk[c] + jnp.where(pair_diag, upd, 0.0)
    state_out.value = state


def _run_interleaved(chains):
    chains = list(chains)
    while chains:
        alive = []
        for ch in chains:
            try:
                next(ch)
                alive.append(ch)
            except StopIteration:
                pass
        chains = alive


def _scan_kernel(*refs, has_init, want_state, groups):
    fwd_refs, bwd_refs = refs[:6], refs[6:12]
    ka_ref = refs[12]
    pos = 13
    s0_ref = None
    if has_init:
        s0_ref = refs[pos]
        pos += 1
    yf_ref, yb_ref = refs[pos:pos + 2]
    pos += 2
    so_ref = None
    if want_state:
        so_ref = refs[pos]
        pos += 1
    st_ref = refs[pos]
    i = pl.program_id(2)

    @pl.when(i == 0)
    def _():
        if has_init:
            st_ref[...] = s0_ref[0]
        else:
            st_ref[...] = jnp.zeros(st_ref.shape, F32)

    blocks = yf_ref.shape[1] // TBLK
    chains, finals = [], []
    for g in range(groups):
        ls = slice(g * LANES, (g + 1) * LANES)
        k_a = ka_ref[:, ls]
        for z, (refs_z, y_ref) in enumerate(((fwd_refs, yf_ref), (bwd_refs, yb_ref))):
            carry = _Handoff()
            carry.value = st_ref[z, g]
            for j in range(blocks):
                row0 = (blocks - 1 - j if z else j) * TBLK
                nxt = _Handoff()
                chains.append(_scan_chain(refs_z, row0, ls, k_a, bool(z), carry, nxt, y_ref,
                                          j * STATE_FREE_ROUNDS))
                carry = nxt
            finals.append(((z, g), carry))
    _run_interleaved(chains)
    for idx, cell in finals:
        st_ref[idx] = cell.value

    if want_state:
        @pl.when(i == pl.num_programs(2) - 1)
        def _():
            for z in range(2):
                for g in range(groups):
                    s = st_ref[z, g]
                    so_ref[0, z, 2 * g] = s[:HEAD, :HEAD]
                    so_ref[0, z, 2 * g + 1] = s[HEAD:, HEAD:]


def _rwkv_scan(r, k, v, kk, cl_f, cl_b, a_f, a_b, k_a, s0, want_state):
    b, t, width = r.shape
    pairs = width // LANES
    g = SCAN_PAIRS
    gw = g * LANES
    rows = SCAN_BLOCKS * TBLK
    nt = t // rows
    fwd = pl.BlockSpec((1, rows, gw), lambda bi, p, i: (bi, i, p))
    bwd = pl.BlockSpec((1, rows, gw), lambda bi, p, i: (bi, nt - 1 - i, p))
    st_spec = pl.BlockSpec((1, 2, g, LANES, LANES), lambda bi, p, i: (bi, 0, p, 0, 0))
    specs = [fwd] * 6 + [bwd] * 6 + [pl.BlockSpec((1, gw), lambda bi, p, i: (0, p))]
    ins = [r, k, v, kk, cl_f, a_f, r, k, v, kk, cl_b, a_b, k_a.reshape(1, width)]
    if s0 is not None:
        specs.append(st_spec)
        ins.append(s0)
    y_shape = jax.ShapeDtypeStruct((b, t, width), F32)
    out_shape = [y_shape, y_shape]
    out_specs = [fwd, bwd]
    if want_state:
        out_shape.append(jax.ShapeDtypeStruct((b, 2, 2 * pairs, HEAD, HEAD), F32))
        out_specs.append(pl.BlockSpec((1, 2, 2 * g, HEAD, HEAD), lambda bi, p, i: (bi, 0, p, 0, 0)))
    kern = functools.partial(_scan_kernel, has_init=s0 is not None, want_state=want_state, groups=g)
    return pl.pallas_call(
        kern,
        out_shape=out_shape,
        grid=(b, pairs // g, nt),
        in_specs=specs,
        out_specs=out_specs,
        scratch_shapes=[pltpu.VMEM((2, g, LANES, LANES), F32)],
        compiler_params=_cparams(("arbitrary", "arbitrary", "arbitrary")),
        name="rwkv7_scan",
    )(*ins)


def _pair_states(s):
    b, z, h, n, _ = s.shape
    sp = s.reshape(b, z, h // 2, 2, n, n)
    zero = jnp.zeros_like(sp[:, :, :, 0])
    top = jnp.concatenate([sp[:, :, :, 0], zero], axis=-1)
    bot = jnp.concatenate([zero, sp[:, :, :, 1]], axis=-1)
    return jnp.concatenate([top, bot], axis=-2)


def _post_kernel(yf_ref, yb_ref, r_ref, k_ref, v_ref, gf_ref, gr_ref, g0_ref, g1_ref, fo_ref, x_ref, mod_ref,
                 vec_ref, seg_ref, wf_ref, wr_ref, wo_ref, o_ref, z_ref, *, cw):
    width = yf_ref.shape[2]
    inv_n = 1.0 / HEAD
    for j in range(width // cw):
        sl = slice(j * cw, (j + 1) * cw)
        seg = seg_ref[...]
        y = yf_ref[0, :, sl] + yb_ref[0, :, sl]
        mean = _dot_exact_rhs(y, seg) * inv_n
        d = y - mean
        var = _dot_exact_rhs(d * d, seg) * inv_n
        yn = d * lax.rsqrt(var + GN_EPS) * vec_ref[0:1, sl] + vec_ref[1:2, sl]
        rk = _dot_exact_rhs(r_ref[0, :, sl] * k_ref[0, :, sl] * vec_ref[2:3, sl], seg)
        z = (yn + rk * v_ref[0, :, sl]) * _silu(gr_ref[0, :, sl])
        z_ref[:, sl] = z.astype(BF16)
    out_r = jnp.dot(z_ref[...], wr_ref[...], preferred_element_type=F32)
    zf = (fo_ref[0] * _silu(gf_ref[0])).astype(BF16)
    out_f = jnp.dot(zf, wf_ref[...], preferred_element_type=F32)
    merged = jax.nn.sigmoid(g0_ref[0]) * out_f + jax.nn.sigmoid(g1_ref[0]) * out_r
    o = jnp.dot(merged.astype(BF16), wo_ref[...], preferred_element_type=F32)
    xn = x_ref[0] + mod_ref[0][2:3] * o
    o_ref[0] = xn * lax.rsqrt(jnp.mean(xn * xn, axis=-1, keepdims=True) + RMS_EPS) * vec_ref[3:4, :]


def _post(yf, yb, r, k, v, u, fo, x, mod, lnx_g, lnx_b, r_k, final_g, w_proj_f, w_proj_r, w_out, tm):
    b, t, d = x.shape
    width = yf.shape[2]
    fw = fo.shape[2]
    cw = MXU_DIM
    vec = jnp.stack([lnx_g, lnx_b, r_k.reshape(-1), final_g], axis=0)
    tok = lambda w, c: pl.BlockSpec((1, tm, w), lambda bi, i: (bi, i, c))
    const = lambda shape: pl.BlockSpec(shape, lambda bi, i: (0,) * len(shape))
    gr_off = 2 * fw + 3 * width
    mg_off = gr_off + width
    assert gr_off % width == 0 and mg_off % d == 0
    gr_blk, mg_blk = gr_off // width, mg_off // d
    specs = [tok(width, 0)] * 5 + [tok(fw, 1), tok(width, gr_blk), tok(d, mg_blk), tok(d, mg_blk + 1),
                                   tok(fw, 0), tok(d, 0), pl.BlockSpec((1, 8, d), lambda bi, i: (bi, 0, 0)),
                                   const((4, d)), const((cw, cw)), const((fw, d)), const((width, d)),
                                   const((d, d))]
    return pl.pallas_call(
        functools.partial(_post_kernel, cw=cw),
        out_shape=jax.ShapeDtypeStruct((b, t, d), F32),
        grid=(b, t // tm),
        in_specs=specs,
        out_specs=tok(d, 0),
        scratch_shapes=[pltpu.VMEM((tm, width), BF16)],
        compiler_params=_cparams(("arbitrary", "arbitrary")),
        name="post_mix",
    )(yf, yb, r, k, v, u, u, u, u, fo, x, mod, vec, _seg_matrix(cw),
      w_proj_f.astype(BF16), w_proj_r.astype(BF16), w_out.astype(BF16))


def _mixer_path(x, mod, s0, want_state, grid_shift, p, tiles):
    b, t, d = x.shape
    u = _inproj(x, mod, p["norm_g"], p["w_in"], tiles["inproj"], p["w_in_segments"])
    fo = _fourier(u, t, p["fourier_width"], tiles["fourier_n1"], tiles["fourier"])
    r, k, v, kk, cl_f, cl_b, a_f, a_b = _rwkv_prep(
        u, p["mu_shift"], p["k_k"], p["w_up"], p["a_up"], p["w0"], p["a0"], grid_shift, tiles["prep"])
    res = _rwkv_scan(r, k, v, kk, cl_f, cl_b, a_f, a_b, p["k_a"], s0, want_state)
    y = _post(res[0], res[1], r, k, v, u, fo, x, mod, p["lnx_g"], p["lnx_b"], p["r_k"], p["final_g"],
              p["w_proj_f"], p["w_proj_r"], p["w_out"], tiles["post"])
    return y, (res[2] if want_state else None)


def _tiles(t, grid_shift):
    if not grid_shift:
        return dict(inproj=t, fourier=t, fourier_n1=1, prep=t, post=t)
    return dict(inproj=512, fourier=256, fourier_n1=4, prep=256, post=256)


def kernel(x_prompt, x_sample, state_rwkv, c, c_ctx, norm_g, w_ada, b_ada, w_in, mu_shift, w0, w_up, a0,
           a_up, k_k, k_a, r_k, lnx_g, lnx_b, w_proj_f, w_proj_r, w_out, final_g):
    depth = w_in.shape[0]
    assert depth == 1, "the final norm is fused into the single layer's post kernel"
    bp, tp, d = x_prompt.shape
    bs, ts, _ = x_sample.shape
    width = k_k.shape[1]
    fw = w_proj_f.shape[1]
    l = 0
    sh_end = 2 * fw + 3 * width
    rank2 = mu_shift.shape[1] - 3 * width
    n_in = w_in.shape[2]
    segments = ((0, 0, sh_end), (sh_end + rank2, sh_end, n_in - sh_end - rank2), (sh_end, n_in - rank2, rank2))
    p = dict(norm_g=norm_g[l], w_in=w_in[l].astype(BF16), w_in_segments=segments, fourier_width=fw,
             mu_shift=mu_shift[l], k_k=k_k[l], w_up=w_up[l], a_up=a_up[l], w0=w0[l], a0=a0[l], k_a=k_a[l],
             lnx_g=lnx_g[l], lnx_b=lnx_b[l], r_k=r_k[l], final_g=final_g, w_proj_f=w_proj_f[l],
             w_proj_r=w_proj_r[l], w_out=w_out[l])

    rows = 8
    cvec = jnp.concatenate([c_ctx[None], c, jnp.zeros((rows - 1 - bs, d), F32)], axis=0)
    m = _modulation(cvec, w_ada[l], b_ada[l])
    m3 = m.reshape(rows, 3, d)
    mod = jnp.concatenate([m3, jnp.zeros((rows, 5, d), F32)], axis=1)

    yp, sp = _mixer_path(x_prompt, jnp.broadcast_to(mod[0:1], (bp, 8, d)), None, True, False, p,
                         _tiles(tp, False))
    s0 = _pair_states(state_rwkv[:, l])
    ys, _ = _mixer_path(x_sample, mod[1:1 + bs], s0, False, True, p, _tiles(ts, True))
    new_state = sp[:, None]
    return yp, ys, new_state
```

```python
import functools
import math

import numpy as np
import jax
import jax.numpy as jnp
from jax import lax
from jax.experimental import pallas as pl
from jax.experimental.pallas import tpu as pltpu

F32 = jnp.float32
BF16 = jnp.bfloat16

LANES = 128
MXU_DIM = 256
VMEM_LIMIT = 56 * 1024 * 1024

HEAD = 64
GRID_W = 64
FOURIER_GROUP = 128
RMS_EPS = 1e-6
GN_EPS = 64e-5

TBLK = 128
CHUNK = 64
SCAN_PAIRS = 8
SCAN_BLOCKS = 1
STATE_FREE_ROUNDS = int(math.log2(CHUNK)) + 3


def _cparams(sem):
    return pltpu.CompilerParams(dimension_semantics=sem, vmem_limit_bytes=VMEM_LIMIT)


def _bdot(a, b):
    return jnp.dot(a.astype(BF16), b.astype(BF16), preferred_element_type=F32)


def _bdot_nt(a, b):
    return lax.dot_general(a.astype(BF16), b.astype(BF16), (((1,), (1,)), ((), ())),
                           preferred_element_type=F32)


def _bdot_tn(a, b):
    return lax.dot_general(a.astype(BF16), b.astype(BF16), (((0,), (0,)), ((), ())),
                           preferred_element_type=F32)


def _split2(x):
    hi = x.astype(BF16)
    lo = (x - hi.astype(F32)).astype(BF16)
    return hi, lo


def _dot_exact_rhs(x, m):
    hi, lo = _split2(x)
    return (jnp.dot(hi, m, preferred_element_type=F32) + jnp.dot(lo, m, preferred_element_type=F32))


def _silu(x):
    return x * jax.nn.sigmoid(x)


def _mod_kernel(c_ref, w_ref, b_ref, o_ref):
    c = c_ref[...]
    o_ref[...] = jnp.dot(_silu(c), w_ref[...], precision=lax.Precision.HIGHEST,
                         preferred_element_type=F32) + b_ref[...]


def _modulation(cvec, w_ada, b_ada):
    rows, d = cvec.shape
    n = w_ada.shape[1]
    tn = n // 2
    return pl.pallas_call(
        _mod_kernel,
        out_shape=jax.ShapeDtypeStruct((rows, n), F32),
        grid=(n // tn,),
        in_specs=[pl.BlockSpec((rows, d), lambda j: (0, 0)),
                  pl.BlockSpec((d, tn), lambda j: (0, j)),
                  pl.BlockSpec((1, tn), lambda j: (0, j))],
        out_specs=pl.BlockSpec((rows, tn), lambda j: (0, j)),
        compiler_params=_cparams(("arbitrary",)),
        name="adaln_modulation",
    )(cvec, w_ada, b_ada.reshape(1, n))


def _inproj_kernel(x_ref, mod_ref, g_ref, w_ref, o_ref, *, tn, segments):
    x = x_ref[0]
    y = x * lax.rsqrt(jnp.mean(x * x, axis=-1, keepdims=True) + RMS_EPS) * g_ref[...]
    m = mod_ref[0]
    h = (y * (1.0 + m[1:2]) + m[0:1]).astype(BF16)
    for src, dst, size in segments:
        for j0 in range(0, size, tn):
            w = min(tn, size - j0)
            o_ref[0, :, dst + j0:dst + j0 + w] = jnp.dot(h, w_ref[:, src + j0:src + j0 + w],
                                                         preferred_element_type=F32)


def _inproj(x, mod, norm_g, w_bf16, tm, segments):
    b, t, d = x.shape
    n = w_bf16.shape[1]
    return pl.pallas_call(
        functools.partial(_inproj_kernel, tn=2 * MXU_DIM, segments=segments),
        out_shape=jax.ShapeDtypeStruct((b, t, n), F32),
        grid=(b, t // tm),
        in_specs=[pl.BlockSpec((1, tm, d), lambda bi, i: (bi, i, 0)),
                  pl.BlockSpec((1, 8, d), lambda bi, i: (bi, 0, 0)),
                  pl.BlockSpec((1, d), lambda bi, i: (0, 0)),
                  pl.BlockSpec((d, n), lambda bi, i: (0, 0), pipeline_mode=pl.Buffered(1))],
        out_specs=pl.BlockSpec((1, tm, n), lambda bi, i: (bi, i, 0)),
        compiler_params=_cparams(("arbitrary", "arbitrary")),
        name="norm_mod_inproj",
    )(x, mod, norm_g.reshape(1, d), w_bf16)


_ROT = ((1.0, 0, -1.0, 1), (-1.0, 1, -1.0, 0), (-1.0, 0, 1.0, 1), (1.0, 1, 1.0, 0))


def _fourier_kernel(xf_ref, cc_ref, er_ref, ei_ref, o_ref, g_ref, il_ref, *, n1, t2, rows):
    width = xf_ref.shape[2]

    @pl.when(pl.program_id(1) == 0)
    def _():
        def body(i, carry):
            r0 = pl.multiple_of(i * rows, rows)
            pq = []
            for a in range(n1):
                x = xf_ref[0, pl.ds(a * t2 + r0, rows), :]
                parts = [_bdot(x[:, g * FOURIER_GROUP:(g + 1) * FOURIER_GROUP], cc_ref[...])
                         for g in range(width // FOURIER_GROUP)]
                pq.append((jnp.concatenate([p[:, :FOURIER_GROUP] for p in parts], axis=1),
                           jnp.concatenate([p[:, FOURIER_GROUP:] for p in parts], axis=1)))
            for f1 in range(n1):
                re = im = None
                for a in range(n1):
                    sr, cr, si, ci = _ROT[(a * f1) % 4]
                    tr, ti = sr * pq[a][cr], si * pq[a][ci]
                    re = tr if re is None else re + tr
                    im = ti if im is None else im + ti
                g_ref[f1, pl.ds(r0, rows), :] = re.astype(BF16)
                g_ref[f1, pl.ds(t2 + r0, rows), :] = im.astype(BF16)
            return carry
        lax.fori_loop(0, t2 // rows, body, 0)

    tf = er_ref.shape[1]
    for f1 in range(n1):
        res = (jnp.dot(er_ref[f1], g_ref[f1, 0:t2], preferred_element_type=F32)
               + jnp.dot(ei_ref[f1], g_ref[f1, t2:2 * t2], preferred_element_type=F32))
        if n1 == 1:
            o_ref[0] = res
        else:
            for g in range(width // LANES):
                il_ref[g, pl.ds(f1, tf, stride=n1), :] = res[:, g * LANES:(g + 1) * LANES]
    if n1 > 1:
        for g in range(width // LANES):
            o_ref[0, :, g * LANES:(g + 1) * LANES] = il_ref[g]


def _dft_consts(t, n1):
    t2 = t // n1
    g = FOURIER_GROUP
    j = np.arange(g)
    ang = 2.0 * np.pi * ((j[:, None] * j[None, :]) % g) / g
    cc = jnp.asarray(np.concatenate([np.cos(ang), np.sin(ang)], axis=1) / math.sqrt(g), dtype=F32).astype(BF16)
    scale = 1.0 / math.sqrt(t)
    if n1 == 1:
        k = np.arange(t)
        ang = 2.0 * np.pi * ((k[:, None] * k[None, :]) % t) / t
        tab = lambda m: jnp.asarray(m[None] * scale, dtype=F32).astype(BF16)
        return tab(np.cos(ang)), tab(np.sin(ang)), cc
    sa = 1 << (int(math.log2(t2)) // 2)
    sb = t2 // sa
    assert sa * sb == t2 and n1 * t2 == t
    tt = np.arange(t2)
    f1 = np.arange(n1)[:, None, None]
    bb = np.arange(sb)[None, :, None]
    ang_x = 2.0 * np.pi * ((tt[None, None, :] * (f1 + n1 * bb)) % t) / t
    aa = np.arange(sa)[:, None]
    ang_y = 2.0 * np.pi * ((tt[None, :] * aa) % sa) / sa
    xr, xi = jnp.asarray(np.cos(ang_x) * scale, F32), jnp.asarray(-np.sin(ang_x) * scale, F32)
    yr, yi = jnp.asarray(np.cos(ang_y), F32), jnp.asarray(-np.sin(ang_y), F32)
    xr, xi = xr[:, None], xi[:, None]
    yr, yi = yr[None, :, None], yi[None, :, None]
    re = (xr * yr - xi * yi).reshape(n1, t2, t2)
    im = (xr * yi + xi * yr).reshape(n1, t2, t2)
    return re.astype(BF16), (-im).astype(BF16), cc


def _fourier(u, t, width, n1, tf):
    b = u.shape[0]
    t2 = t // n1
    e_re, e_nim, cc = _dft_consts(t, n1)
    e_spec = pl.BlockSpec((n1, tf, t2), lambda bi, i: (0, i, 0))
    kern = functools.partial(_fourier_kernel, n1=n1, t2=t2, rows=min(t2, 256))
    return pl.pallas_call(
        kern,
        out_shape=jax.ShapeDtypeStruct((b, t, width), F32),
        grid=(b, t2 // tf),
        in_specs=[pl.BlockSpec((1, t, width), lambda bi, i: (bi, 0, 0)),
                  pl.BlockSpec((FOURIER_GROUP, 2 * FOURIER_GROUP), lambda bi, i: (0, 0)),
                  e_spec, e_spec],
        out_specs=pl.BlockSpec((1, n1 * tf, width), lambda bi, i: (bi, i, 0)),
        scratch_shapes=[pltpu.VMEM((n1, 2 * t2, width), BF16),
                        pltpu.VMEM((width // LANES, n1 * tf if n1 > 1 else 8, LANES), F32)],
        compiler_params=_cparams(("arbitrary", "arbitrary")),
        name="fourier_mix",
    )(u, cc, e_re, e_nim)


def _shift(x, prev_halo, next_halo, t_total, row0, grid_shift):
    rows, lanes = x.shape
    row = lax.broadcasted_iota(jnp.int32, (rows, lanes), 0)
    lane = lax.broadcasted_iota(jnp.int32, (rows, lanes), 1)
    before = pltpu.roll(x, 1, 0)
    after = pltpu.roll(x, rows - 1, 0)
    if not grid_shift:
        even = (lane & 1) == 0
        outside = (even & (row == 0)) | (~even & (row == rows - 1))
        return jnp.where(outside, 0.0, jnp.where(even, before, after))
    col = row & (GRID_W - 1)
    grow = row + row0
    up = jnp.concatenate([prev_halo, x[:rows - GRID_W]], axis=0)
    down = jnp.concatenate([x[GRID_W:], next_halo], axis=0)
    m = lane & 3
    outside = (((m == 0) & (col == 0)) | ((m == 1) & (col == GRID_W - 1))
               | ((m == 2) & (grow < GRID_W)) | ((m == 3) & (grow >= t_total - GRID_W)))
    pick = jnp.where(m == 0, before, jnp.where(m == 1, after, jnp.where(m == 2, up, down)))
    return jnp.where(outside, 0.0, pick)


def _cumsum_matrices(rows):
    t = np.arange(rows)
    same = (t[:, None] // CHUNK) == (t[None, :] // CHUNK)
    tri = np.stack([same & (t[None, :] <= t[:, None]), same & (t[None, :] >= t[:, None])])
    return jnp.asarray(tri.astype(np.float32), dtype=BF16)


def _prep_kernel(*refs, grid_shift, t_total, width, cw):
    if grid_shift:
        (r_ref, k_ref, v_ref, d_ref, rp_ref, kp_ref, vp_ref, dp_ref, rn_ref, kn_ref, vn_ref, dn_ref,
         mu_ref, mud_ref, kk_ref, wcat_ref, bias_ref, seg_ref, tri_ref,
         ro_ref, ko_ref, vo_ref, kko_ref, lwf_ref, lwb_ref, af_ref, ab_ref) = refs
    else:
        (r_ref, k_ref, v_ref, d_ref, mu_ref, mud_ref, kk_ref, wcat_ref, bias_ref, seg_ref, tri_ref,
         ro_ref, ko_ref, vo_ref, kko_ref, lwf_ref, lwb_ref, af_ref, ab_ref) = refs
        rp_ref = kp_ref = vp_ref = dp_ref = rn_ref = kn_ref = vn_ref = dn_ref = None
    rows = r_ref.shape[1]
    row0 = pl.program_id(1) * rows

    def lerp(x_ref, p_ref, n_ref, mu, sl):
        x = x_ref[0, :, sl]
        ph = p_ref[0, :, sl] if grid_shift else None
        nh = n_ref[0, :, sl] if grid_shift else None
        s = _shift(x, ph, nh, t_total, row0, grid_shift)
        return x + mu * (s - x)

    dsl = slice(0, LANES)
    d = lerp(d_ref, dp_ref, dn_ref, mud_ref[...], dsl)
    lane = lax.broadcasted_iota(jnp.int32, d.shape, 1)
    d = jnp.where(lane < HEAD, jnp.tanh(d), d).astype(BF16)
    outs = (lwf_ref, lwb_ref, af_ref, ab_ref)
    for j in range(width // cw):
        sl = slice(j * cw, (j + 1) * cw)
        ro_ref[0, :, sl] = lerp(r_ref, rp_ref, rn_ref, mu_ref[0:1, sl], sl)
        vo_ref[0, :, sl] = lerp(v_ref, vp_ref, vn_ref, mu_ref[2:3, sl], sl)
        k = lerp(k_ref, kp_ref, kn_ref, mu_ref[1:2, sl], sl)
        ko_ref[0, :, sl] = k
        kk = k * kk_ref[0:1, sl]
        ss = _dot_exact_rhs(kk * kk, seg_ref[...])
        kko_ref[0, :, sl] = kk / jnp.maximum(jnp.sqrt(ss), 1e-12)
        for z in range(4):
            wsl = slice(z * width + j * cw, z * width + (j + 1) * cw)
            pre = jnp.dot(d, wcat_ref[:, wsl], preferred_element_type=F32) + bias_ref[z:z + 1, sl]
            sg = jax.nn.sigmoid(pre)
            if z < 2:
                hi, lo = _split2(sg * (-math.exp(-0.5)))
                outs[z][0, :, sl] = (jnp.dot(tri_ref[z], hi, preferred_element_type=F32)
                                     + jnp.dot(tri_ref[z], lo, preferred_element_type=F32))
            else:
                outs[z][0, :, sl] = sg


def _rwkv_prep(u, mu_shift, k_k, w_up, a_up, w0, a0, grid_shift, tt):
    b, t, _ = u.shape
    width = k_k.shape[0]
    cw = MXU_DIM
    mu = mu_shift[:3 * width].reshape(3, width)
    mud = mu_shift[3 * width:].reshape(1, LANES)
    zpad = jnp.zeros((HEAD, width), F32)
    wcat = jnp.concatenate([jnp.concatenate([w_up[0], zpad], 0), jnp.concatenate([w_up[1], zpad], 0),
                            jnp.concatenate([zpad, a_up[0]], 0), jnp.concatenate([zpad, a_up[1]], 0)],
                           axis=1).astype(BF16)
    bias = jnp.concatenate([w0, a0], axis=0)
    seg = _seg_matrix(cw)
    d_blk = (u.shape[2] - LANES) // LANES
    main = [pl.BlockSpec((1, tt, width), lambda bi, i, c=c: (bi, i, c)) for c in (1, 2, 3)]
    main.append(pl.BlockSpec((1, tt, LANES), lambda bi, i: (bi, i, d_blk)))
    ins = [u, u, u, u]
    specs = list(main)
    if grid_shift:
        hb = tt // GRID_W
        last = t // GRID_W - 1
        for off in (-1, hb):
            def imap(bi, i, c, off=off):
                return (bi, jnp.clip(i * hb + off, 0, last), c)
            specs += [pl.BlockSpec((1, GRID_W, width), functools.partial(imap, c=c)) for c in (1, 2, 3)]
            specs.append(pl.BlockSpec((1, GRID_W, LANES), functools.partial(imap, c=d_blk)))
            ins += [u, u, u, u]
    const = lambda shape: pl.BlockSpec(shape, lambda bi, i: (0, 0))
    specs += [const((3, width)), const((1, LANES)), const((1, width)), const((LANES, 4 * width)),
              const((4, width)), const((cw, cw)), pl.BlockSpec((2, tt, tt), lambda bi, i: (0, 0, 0))]
    ins += [mu, mud, k_k.reshape(1, width), wcat, bias, seg, _cumsum_matrices(tt)]
    out_spec = pl.BlockSpec((1, tt, width), lambda bi, i: (bi, i, 0))
    kern = functools.partial(_prep_kernel, grid_shift=grid_shift, t_total=t, width=width, cw=cw)
    return pl.pallas_call(
        kern,
        out_shape=[jax.ShapeDtypeStruct((b, t, width), F32)] * 8,
        grid=(b, t // tt),
        in_specs=specs,
        out_specs=[out_spec] * 8,
        compiler_params=_cparams(("arbitrary", "arbitrary")),
        name="rwkv_prep",
    )(*ins)


def _seg_matrix(n):
    i = np.arange(n) // HEAD
    return jnp.asarray((i[:, None] == i[None, :]).astype(np.float32), dtype=BF16)


def _heads_stacked(x, m0):
    return jnp.concatenate([jnp.where(m0, x, 0.0), jnp.where(m0, 0.0, x)], axis=0)


class _Handoff:
    value = None


def _scan_chain(in_refs, row0, ls, k_a, reverse, state_in, state_out, y_ref, delay):
    for _ in range(delay):
        yield
    r_ref, k_ref, v_ref, kk_ref, cl_ref, a_ref = in_refs
    n = TBLK
    rs = slice(row0, row0 + n)
    cl = cl_ref[0, rs, ls]
    pos = lax.broadcasted_iota(jnp.int32, (n, LANES), 0) & (CHUNK - 1)
    if reverse:
        cl_ex = jnp.where(pos == CHUNK - 1, 0.0, pltpu.roll(cl, n - 1, 0))
    else:
        cl_ex = jnp.where(pos == 0, 0.0, pltpu.roll(cl, 1, 0))
    n_chunks = n // CHUNK
    ends = [c * CHUNK if reverse else (c + 1) * CHUNK - 1 for c in range(n_chunks)]
    tot_rows = [cl[e:e + 1] for e in ends]
    tot = jnp.concatenate([jnp.broadcast_to(tr, (CHUNK, LANES)) for tr in tot_rows], axis=0)
    r, k, v, kk, a = r_ref[0, rs, ls], k_ref[0, rs, ls], v_ref[0, rs, ls], kk_ref[0, rs, ls], a_ref[0, rs, ls]
    kd = k * (1.0 + (a - 1.0) * k_a)
    b = kk * a
    at = -kk * jnp.exp(cl_ex)
    rt = r * jnp.exp(cl)
    e_neg = jnp.exp(-cl)
    e_tail = jnp.exp(tot - cl)
    bh = b * e_tail
    kh = kd * e_tail
    w_chunk = [jnp.exp(tr) for tr in tot_rows]

    m0 = lax.broadcasted_iota(jnp.int32, (n, LANES), 1) < HEAD
    lhs = jnp.concatenate([_heads_stacked(at, m0), _heads_stacked(rt, m0)], axis=0)
    rhs = jnp.concatenate([b * e_neg, kd * e_neg], axis=0)
    g = _bdot_nt(lhs, rhs)
    yield

    row = lax.broadcasted_iota(jnp.int32, (n, 2 * n), 0)
    col = lax.broadcasted_iota(jnp.int32, (n, 2 * n), 1) & (n - 1)
    same = (row // CHUNK) == (col // CHUNK)
    if reverse:
        strict, incl = same & (col > row), same & (col >= row)
    else:
        strict, incl = same & (col < row), same & (col <= row)
    pick = lambda r0, cs: jnp.concatenate([g[r0:r0 + n, cs], g[r0 + n:r0 + 2 * n, cs]], axis=1)
    left, right = slice(0, n), slice(n, 2 * n)
    a_ab = jnp.where(strict, pick(0, left), 0.0)
    a_kr = jnp.concatenate([jnp.where(strict, pick(0, right), 0.0),
                            jnp.where(incl, pick(2 * n, right), 0.0)], axis=0).astype(BF16)
    a_rb = jnp.where(incl, pick(2 * n, left), 0.0).astype(BF16)

    blocks = 2 * n_chunks
    brow = lax.broadcasted_iota(jnp.int32, (2 * n, 2 * n), 0)
    bcol = lax.broadcasted_iota(jnp.int32, (2 * n, 2 * n), 1)
    on_block = (brow // CHUNK) == (bcol // CHUNK)
    spread = lambda x: jnp.where(on_block, jnp.concatenate([x.astype(BF16)] * blocks, axis=0), 0.0)
    steps = int(math.log2(CHUNK)) - 1
    q = a_ab[:CHUNK]
    for c in range(1, n_chunks):
        q = q + a_ab[c * CHUNK:(c + 1) * CHUNK]
    p = _bdot(q, spread(q))
    kr = jnp.dot(a_kr, _heads_stacked(v, m0).astype(BF16), preferred_element_type=F32)
    yield
    akv, rkv = kr[:n], kr[n:]
    for _ in range(steps - 1):
        res = _bdot(jnp.concatenate([p, q], axis=0), spread(p))
        yield
        q = q + p + res[CHUNK:]
        p = res[:CHUNK]
    fin = _bdot(q, spread(p))
    yield
    q = q + p + fin
    q16 = jnp.where(same, jnp.concatenate([q] * n_chunks, axis=0), 0.0).astype(BF16)

    qx = jnp.dot(q16, jnp.concatenate([_heads_stacked(at, m0), _heads_stacked(akv, m0)], axis=1).astype(BF16),
                 preferred_element_type=F32)
    yield
    a_hat = at + qx[:, :LANES]
    uv = akv + qx[:, LANES:]
    both = jnp.dot(a_rb, jnp.concatenate([_heads_stacked(a_hat, m0), _heads_stacked(uv, m0)],
                                         axis=1).astype(BF16), preferred_element_type=F32)
    yield
    r_hat = rt + both[:, :LANES]
    yv = both[:, LANES:] + rkv

    srow = lax.broadcasted_iota(jnp.int32, (LANES, LANES), 0)
    scol = lax.broadcasted_iota(jnp.int32, (LANES, LANES), 1)
    pair_diag = (srow < HEAD) == (scol < HEAD)
    while state_in.value is None:
        yield
    state = state_in.value
    for c in (range(n_chunks - 1, -1, -1) if reverse else range(n_chunks)):
        sl = slice(c * CHUNK, (c + 1) * CHUNK)
        proj = _bdot_nt(jnp.concatenate([a_hat[sl], r_hat[sl]], axis=0), state)
        yield
        u_c = proj[:CHUNK] + uv[sl]
        y_ref[0, row0 + c * CHUNK:row0 + (c + 1) * CHUNK, ls] = proj[CHUNK:] + yv[sl]
        upd = _bdot_tn(jnp.concatenate([u_c, v[sl]], axis=0), jnp.concatenate([bh[sl], kh[sl]], axis=0))
        yield
        state = state * w_chunk[c] + jnp.where(pair_diag, upd, 0.0)
    state_out.value = state


def _run_interleaved(chains):
    chains = list(chains)
    while chains:
        alive = []
        for ch in chains:
            try:
                next(ch)
                alive.append(ch)
            except StopIteration:
                pass
        chains = alive


def _scan_kernel(*refs, has_init, want_state, groups):
    fwd_refs, bwd_refs = refs[:6], refs[6:12]
    ka_ref = refs[12]
    pos = 13
    s0_ref = None
    if has_init:
        s0_ref = refs[pos]
        pos += 1
    yf_ref, yb_ref = refs[pos:pos + 2]
    pos += 2
    so_ref = None
    if want_state:
        so_ref = refs[pos]
        pos += 1
    st_ref = refs[pos]
    i = pl.program_id(2)

    @pl.when(i == 0)
    def _():
        if has_init:
            st_ref[...] = s0_ref[0]
        else:
            st_ref[...] = jnp.zeros(st_ref.shape, F32)

    blocks = yf_ref.shape[1] // TBLK
    chains, finals = [], []
    for g in range(groups):
        ls = slice(g * LANES, (g + 1) * LANES)
        k_a = ka_ref[:, ls]
        for z, (refs_z, y_ref) in enumerate(((fwd_refs, yf_ref), (bwd_refs, yb_ref))):
            carry = _Handoff()
            carry.value = st_ref[z, g]
            for j in range(blocks):
                row0 = (blocks - 1 - j if z else j) * TBLK
                nxt = _Handoff()
                chains.append(_scan_chain(refs_z, row0, ls, k_a, bool(z), carry, nxt, y_ref,
                                          j * STATE_FREE_ROUNDS))
                carry = nxt
            finals.append(((z, g), carry))
    _run_interleaved(chains)
    for idx, cell in finals:
        st_ref[idx] = cell.value

    if want_state:
        @pl.when(i == pl.num_programs(2) - 1)
        def _():
            for z in range(2):
                for g in range(groups):
                    s = st_ref[z, g]
                    so_ref[0, z, 2 * g] = s[:HEAD, :HEAD]
                    so_ref[0, z, 2 * g + 1] = s[HEAD:, HEAD:]


def _rwkv_scan(r, k, v, kk, cl_f, cl_b, a_f, a_b, k_a, s0, want_state):
    b, t, width = r.shape
    pairs = width // LANES
    g = SCAN_PAIRS
    gw = g * LANES
    rows = SCAN_BLOCKS * TBLK
    nt = t // rows
    fwd = pl.BlockSpec((1, rows, gw), lambda bi, p, i: (bi, i, p))
    bwd = pl.BlockSpec((1, rows, gw), lambda bi, p, i: (bi, nt - 1 - i, p))
    st_spec = pl.BlockSpec((1, 2, g, LANES, LANES), lambda bi, p, i: (bi, 0, p, 0, 0))
    specs = [fwd] * 6 + [bwd] * 6 + [pl.BlockSpec((1, gw), lambda bi, p, i: (0, p))]
    ins = [r, k, v, kk, cl_f, a_f, r, k, v, kk, cl_b, a_b, k_a.reshape(1, width)]
    if s0 is not None:
        specs.append(st_spec)
        ins.append(s0)
    y_shape = jax.ShapeDtypeStruct((b, t, width), F32)
    out_shape = [y_shape, y_shape]
    out_specs = [fwd, bwd]
    if want_state:
        out_shape.append(jax.ShapeDtypeStruct((b, 2, 2 * pairs, HEAD, HEAD), F32))
        out_specs.append(pl.BlockSpec((1, 2, 2 * g, HEAD, HEAD), lambda bi, p, i: (bi, 0, p, 0, 0)))
    kern = functools.partial(_scan_kernel, has_init=s0 is not None, want_state=want_state, groups=g)
    return pl.pallas_call(
        kern,
        out_shape=out_shape,
        grid=(b, pairs // g, nt),
        in_specs=specs,
        out_specs=out_specs,
        scratch_shapes=[pltpu.VMEM((2, g, LANES, LANES), F32)],
        compiler_params=_cparams(("arbitrary", "arbitrary", "arbitrary")),
        name="rwkv7_scan",
    )(*ins)


def _pair_states(s):
    b, z, h, n, _ = s.shape
    sp = s.reshape(b, z, h // 2, 2, n, n)
    zero = jnp.zeros_like(sp[:, :, :, 0])
    top = jnp.concatenate([sp[:, :, :, 0], zero], axis=-1)
    bot = jnp.concatenate([zero, sp[:, :, :, 1]], axis=-1)
    return jnp.concatenate([top, bot], axis=-2)


def _post_kernel(yf_ref, yb_ref, r_ref, k_ref, v_ref, gf_ref, gr_ref, g0_ref, g1_ref, fo_ref, x_ref, mod_ref,
                 vec_ref, seg_ref, wf_ref, wr_ref, wo_ref, o_ref, z_ref, *, cw):
    width = yf_ref.shape[2]
    inv_n = 1.0 / HEAD
    for j in range(width // cw):
        sl = slice(j * cw, (j + 1) * cw)
        seg = seg_ref[...]
        y = yf_ref[0, :, sl] + yb_ref[0, :, sl]
        mean = _dot_exact_rhs(y, seg) * inv_n
        d = y - mean
        var = _dot_exact_rhs(d * d, seg) * inv_n
        yn = d * lax.rsqrt(var + GN_EPS) * vec_ref[0:1, sl] + vec_ref[1:2, sl]
        rk = _dot_exact_rhs(r_ref[0, :, sl] * k_ref[0, :, sl] * vec_ref[2:3, sl], seg)
        z = (yn + rk * v_ref[0, :, sl]) * _silu(gr_ref[0, :, sl])
        z_ref[:, sl] = z.astype(BF16)
    out_r = jnp.dot(z_ref[...], wr_ref[...], preferred_element_type=F32)
    zf = (fo_ref[0] * _silu(gf_ref[0])).astype(BF16)
    out_f = jnp.dot(zf, wf_ref[...], preferred_element_type=F32)
    merged = jax.nn.sigmoid(g0_ref[0]) * out_f + jax.nn.sigmoid(g1_ref[0]) * out_r
    o = jnp.dot(merged.astype(BF16), wo_ref[...], preferred_element_type=F32)
    xn = x_ref[0] + mod_ref[0][2:3] * o
    o_ref[0] = xn * lax.rsqrt(jnp.mean(xn * xn, axis=-1, keepdims=True) + RMS_EPS) * vec_ref[3:4, :]


def _post(yf, yb, r, k, v, u, fo, x, mod, lnx_g, lnx_b, r_k, final_g, w_proj_f, w_proj_r, w_out, tm):
    b, t, d = x.shape
    width = yf.shape[2]
    fw = fo.shape[2]
    cw = MXU_DIM
    vec = jnp.stack([lnx_g, lnx_b, r_k.reshape(-1), final_g], axis=0)
    tok = lambda w, c: pl.BlockSpec((1, tm, w), lambda bi, i: (bi, i, c))
    const = lambda shape: pl.BlockSpec(shape, lambda bi, i: (0,) * len(shape), pipeline_mode=pl.Buffered(1))
    gr_off = 2 * fw + 3 * width
    mg_off = gr_off + width
    assert gr_off % width == 0 and mg_off % d == 0
    gr_blk, mg_blk = gr_off // width, mg_off // d
    specs = [tok(width, 0)] * 5 + [tok(fw, 1), tok(width, gr_blk), tok(d, mg_blk), tok(d, mg_blk + 1),
                                   tok(fw, 0), tok(d, 0), pl.BlockSpec((1, 8, d), lambda bi, i: (bi, 0, 0)),
                                   const((4, d)), const((cw, cw)), const((fw, d)), const((width, d)),
                                   const((d, d))]
    return pl.pallas_call(
        functools.partial(_post_kernel, cw=cw),
        out_shape=jax.ShapeDtypeStruct((b, t, d), F32),
        grid=(b, t // tm),
        in_specs=specs,
        out_specs=tok(d, 0),
        scratch_shapes=[pltpu.VMEM((tm, width), BF16)],
        compiler_params=_cparams(("arbitrary", "arbitrary")),
        name="post_mix",
    )(yf, yb, r, k, v, u, u, u, u, fo, x, mod, vec, _seg_matrix(cw),
      w_proj_f.astype(BF16), w_proj_r.astype(BF16), w_out.astype(BF16))


def _mixer_path(x, mod, s0, want_state, grid_shift, p, tiles, shared_mod):
    b, t, d = x.shape
    tokens = (lambda a: a.reshape(1, b * t, a.shape[2])) if shared_mod else (lambda a: a)
    u_tok = _inproj(tokens(x), mod, p["norm_g"], p["w_in"], tiles["inproj"], p["w_in_segments"])
    u = u_tok.reshape(b, t, u_tok.shape[2])
    fo = _fourier(u, t, p["fourier_width"], tiles["fourier_n1"], tiles["fourier"])
    r, k, v, kk, cl_f, cl_b, a_f, a_b = _rwkv_prep(
        u, p["mu_shift"], p["k_k"], p["w_up"], p["a_up"], p["w0"], p["a0"], grid_shift, tiles["prep"])
    res = _rwkv_scan(r, k, v, kk, cl_f, cl_b, a_f, a_b, p["k_a"], s0, want_state)
    y = _post(tokens(res[0]), tokens(res[1]), tokens(r), tokens(k), tokens(v), u_tok, tokens(fo), tokens(x), mod,
              p["lnx_g"], p["lnx_b"], p["r_k"], p["final_g"], p["w_proj_f"], p["w_proj_r"], p["w_out"],
              tiles["post"])
    return y.reshape(b, t, d), (res[2] if want_state else None)


def _tiles(t, grid_shift):
    if not grid_shift:
        return dict(inproj=512, fourier=t, fourier_n1=1, prep=t, post=512)
    return dict(inproj=512, fourier=256, fourier_n1=4, prep=512, post=512)


def kernel(x_prompt, x_sample, state_rwkv, c, c_ctx, norm_g, w_ada, b_ada, w_in, mu_shift, w0, w_up, a0,
           a_up, k_k, k_a, r_k, lnx_g, lnx_b, w_proj_f, w_proj_r, w_out, final_g):
    depth = w_in.shape[0]
    assert depth == 1, "the final norm is fused into the single layer's post kernel"
    bp, tp, d = x_prompt.shape
    bs, ts, _ = x_sample.shape
    width = k_k.shape[1]
    fw = w_proj_f.shape[1]
    l = 0
    sh_end = 2 * fw + 3 * width
    rank2 = mu_shift.shape[1] - 3 * width
    n_in = w_in.shape[2]
    segments = ((0, 0, sh_end), (sh_end + rank2, sh_end, n_in - sh_end - rank2), (sh_end, n_in - rank2, rank2))
    p = dict(norm_g=norm_g[l], w_in=w_in[l].astype(BF16), w_in_segments=segments, fourier_width=fw,
             mu_shift=mu_shift[l], k_k=k_k[l], w_up=w_up[l], a_up=a_up[l], w0=w0[l], a0=a0[l], k_a=k_a[l],
             lnx_g=lnx_g[l], lnx_b=lnx_b[l], r_k=r_k[l], final_g=final_g, w_proj_f=w_proj_f[l],
             w_proj_r=w_proj_r[l], w_out=w_out[l])

    rows = 8
    cvec = jnp.concatenate([c_ctx[None], c, jnp.zeros((rows - 1 - bs, d), F32)], axis=0)
    m = _modulation(cvec, w_ada[l], b_ada[l])
    m3 = m.reshape(rows, 3, d)
    mod = jnp.concatenate([m3, jnp.zeros((rows, 5, d), F32)], axis=1)

    yp, sp = _mixer_path(x_prompt, mod[0:1], None, True, False, p, _tiles(tp, False), shared_mod=True)
    s0 = _pair_states(state_rwkv[:, l])
    ys, _ = _mixer_path(x_sample, mod[1:1 + bs], s0, False, True, p, _tiles(ts, True), shared_mod=False)
    new_state = sp[:, None]
    return yp, ys, new_state
```

```python
import functools
import math

import numpy as np
import jax
import jax.numpy as jnp
from jax import lax
from jax.experimental import pallas as pl
from jax.experimental.pallas import tpu as pltpu

F32 = jnp.float32
BF16 = jnp.bfloat16

SUBLANES = 8
LANES = 128
MXU_DIM = 256
VMEM_LIMIT = 56 * 1024 * 1024

HEAD = 64
GRID_W = 64
FOURIER_GROUP = 128
RMS_EPS = 1e-6
GN_EPS = 64e-5

TBLK = 128
CHUNK = 64
SCAN_PAIRS = 8


def _cparams(sem):
    return pltpu.CompilerParams(dimension_semantics=sem, vmem_limit_bytes=VMEM_LIMIT)


def _bdot(a, b):
    return jnp.dot(a.astype(BF16), b.astype(BF16), preferred_element_type=F32)


def _bdot_nt(a, b):
    return lax.dot_general(a.astype(BF16), b.astype(BF16), (((1,), (1,)), ((), ())),
                           preferred_element_type=F32)


def _bdot_tn(a, b):
    return lax.dot_general(a.astype(BF16), b.astype(BF16), (((0,), (0,)), ((), ())),
                           preferred_element_type=F32)


def _split2(x):
    hi = x.astype(BF16)
    lo = (x - hi.astype(F32)).astype(BF16)
    return hi, lo


def _dot_exact_rhs(x, m):
    hi, lo = _split2(x)
    return (jnp.dot(hi, m, preferred_element_type=F32) + jnp.dot(lo, m, preferred_element_type=F32))


def _silu(x):
    return x * jax.nn.sigmoid(x)


def _mod_kernel(c_ref, w_ref, b_ref, o_ref):
    c = c_ref[...]
    o_ref[...] = jnp.dot(_silu(c), w_ref[...], precision=lax.Precision.HIGHEST,
                         preferred_element_type=F32) + b_ref[...]


def _modulation(cvec, w_ada, b_ada):
    rows, d = cvec.shape
    n = w_ada.shape[1]
    tn = n // 2
    return pl.pallas_call(
        _mod_kernel,
        out_shape=jax.ShapeDtypeStruct((rows, n), F32),
        grid=(n // tn,),
        in_specs=[pl.BlockSpec((rows, d), lambda j: (0, 0)),
                  pl.BlockSpec((d, tn), lambda j: (0, j)),
                  pl.BlockSpec((1, tn), lambda j: (0, j))],
        out_specs=pl.BlockSpec((rows, tn), lambda j: (0, j)),
        compiler_params=_cparams(("arbitrary",)),
        name="adaln_modulation",
    )(cvec, w_ada, b_ada.reshape(1, n))


def _inproj_kernel(x_ref, mod_ref, g_ref, w_ref, o_ref, *, tn, segments):
    x = x_ref[0]
    y = x * lax.rsqrt(jnp.mean(x * x, axis=-1, keepdims=True) + RMS_EPS) * g_ref[...]
    m = mod_ref[0]
    h = (y * (1.0 + m[1:2]) + m[0:1]).astype(BF16)
    for src, dst, size in segments:
        for j0 in range(0, size, tn):
            w = min(tn, size - j0)
            o_ref[0, :, dst + j0:dst + j0 + w] = jnp.dot(h, w_ref[:, src + j0:src + j0 + w],
                                                         preferred_element_type=F32)


def _inproj(x, mod, norm_g, w_bf16, tm, segments):
    b, t, d = x.shape
    n = w_bf16.shape[1]
    return pl.pallas_call(
        functools.partial(_inproj_kernel, tn=2 * MXU_DIM, segments=segments),
        out_shape=jax.ShapeDtypeStruct((b, t, n), F32),
        grid=(b, t // tm),
        in_specs=[pl.BlockSpec((1, tm, d), lambda bi, i: (bi, i, 0)),
                  pl.BlockSpec((1, SUBLANES, d), lambda bi, i: (bi, 0, 0)),
                  pl.BlockSpec((1, d), lambda bi, i: (0, 0)),
                  pl.BlockSpec((d, n), lambda bi, i: (0, 0), pipeline_mode=pl.Buffered(1))],
        out_specs=pl.BlockSpec((1, tm, n), lambda bi, i: (bi, i, 0)),
        compiler_params=_cparams(("arbitrary", "arbitrary")),
        name="norm_mod_inproj",
    )(x, mod, norm_g.reshape(1, d), w_bf16)


_ROT = ((1.0, 0, -1.0, 1), (-1.0, 1, -1.0, 0), (-1.0, 0, 1.0, 1), (1.0, 1, 1.0, 0))


def _fourier_kernel(xf_ref, cc_ref, er_ref, ei_ref, o_ref, g_ref, il_ref, *, n1, t2, rows):
    width = xf_ref.shape[2]

    @pl.when(pl.program_id(1) == 0)
    def _():
        def body(i, carry):
            r0 = pl.multiple_of(i * rows, rows)
            pq = []
            for a in range(n1):
                x = xf_ref[0, pl.ds(a * t2 + r0, rows), :]
                parts = [_bdot(x[:, g * FOURIER_GROUP:(g + 1) * FOURIER_GROUP], cc_ref[...])
                         for g in range(width // FOURIER_GROUP)]
                pq.append((jnp.concatenate([p[:, :FOURIER_GROUP] for p in parts], axis=1),
                           jnp.concatenate([p[:, FOURIER_GROUP:] for p in parts], axis=1)))
            for f1 in range(n1):
                re = im = None
                for a in range(n1):
                    sr, cr, si, ci = _ROT[(a * f1) % 4]
                    tr, ti = sr * pq[a][cr], si * pq[a][ci]
                    re = tr if re is None else re + tr
                    im = ti if im is None else im + ti
                g_ref[f1, pl.ds(r0, rows), :] = re.astype(BF16)
                g_ref[f1, pl.ds(t2 + r0, rows), :] = im.astype(BF16)
            return carry
        lax.fori_loop(0, t2 // rows, body, 0)

    tf = er_ref.shape[1]
    for f1 in range(n1):
        res = (jnp.dot(er_ref[f1], g_ref[f1, 0:t2], preferred_element_type=F32)
               + jnp.dot(ei_ref[f1], g_ref[f1, t2:2 * t2], preferred_element_type=F32))
        if n1 == 1:
            o_ref[0] = res
        else:
            for g in range(width // LANES):
                il_ref[g, pl.ds(f1, tf, stride=n1), :] = res[:, g * LANES:(g + 1) * LANES]
    if n1 > 1:
        for g in range(width // LANES):
            o_ref[0, :, g * LANES:(g + 1) * LANES] = il_ref[g]


def _dft_consts(t, n1):
    t2 = t // n1
    g = FOURIER_GROUP
    j = np.arange(g)
    ang = 2.0 * np.pi * ((j[:, None] * j[None, :]) % g) / g
    cc = jnp.asarray(np.concatenate([np.cos(ang), np.sin(ang)], axis=1) / math.sqrt(g), dtype=F32).astype(BF16)
    scale = 1.0 / math.sqrt(t)
    if n1 == 1:
        k = np.arange(t)
        ang = 2.0 * np.pi * ((k[:, None] * k[None, :]) % t) / t
        tab = lambda m: jnp.asarray(m[None] * scale, dtype=F32).astype(BF16)
        return tab(np.cos(ang)), tab(np.sin(ang)), cc
    sa = 1 << (int(math.log2(t2)) // 2)
    sb = t2 // sa
    assert sa * sb == t2 and n1 * t2 == t
    tt = np.arange(t2)
    f1 = np.arange(n1)[:, None, None]
    bb = np.arange(sb)[None, :, None]
    ang_x = 2.0 * np.pi * ((tt[None, None, :] * (f1 + n1 * bb)) % t) / t
    aa = np.arange(sa)[:, None]
    ang_y = 2.0 * np.pi * ((tt[None, :] * aa) % sa) / sa
    xr, xi = jnp.asarray(np.cos(ang_x) * scale, F32), jnp.asarray(-np.sin(ang_x) * scale, F32)
    yr, yi = jnp.asarray(np.cos(ang_y), F32), jnp.asarray(-np.sin(ang_y), F32)
    xr, xi = xr[:, None], xi[:, None]
    yr, yi = yr[None, :, None], yi[None, :, None]
    re = (xr * yr - xi * yi).reshape(n1, t2, t2)
    im = (xr * yi + xi * yr).reshape(n1, t2, t2)
    return re.astype(BF16), (-im).astype(BF16), cc


def _fourier(u, t, width, n1, tf):
    b = u.shape[0]
    t2 = t // n1
    e_re, e_nim, cc = _dft_consts(t, n1)
    e_spec = pl.BlockSpec((n1, tf, t2), lambda bi, i: (0, i, 0))
    kern = functools.partial(_fourier_kernel, n1=n1, t2=t2, rows=min(t2, MXU_DIM))
    return pl.pallas_call(
        kern,
        out_shape=jax.ShapeDtypeStruct((b, t, width), F32),
        grid=(b, t2 // tf),
        in_specs=[pl.BlockSpec((1, t, width), lambda bi, i: (bi, 0, 0)),
                  pl.BlockSpec((FOURIER_GROUP, 2 * FOURIER_GROUP), lambda bi, i: (0, 0)),
                  e_spec, e_spec],
        out_specs=pl.BlockSpec((1, n1 * tf, width), lambda bi, i: (bi, i, 0)),
        scratch_shapes=[pltpu.VMEM((n1, 2 * t2, width), BF16),
                        pltpu.VMEM((width // LANES, n1 * tf if n1 > 1 else SUBLANES, LANES), F32)],
        compiler_params=_cparams(("arbitrary", "arbitrary")),
        name="fourier_mix",
    )(u, cc, e_re, e_nim)


def _shift(x, prev_halo, next_halo, t_total, row0, grid_shift):
    rows, lanes = x.shape
    row = lax.broadcasted_iota(jnp.int32, (rows, lanes), 0)
    lane = lax.broadcasted_iota(jnp.int32, (rows, lanes), 1)
    before = pltpu.roll(x, 1, 0)
    after = pltpu.roll(x, rows - 1, 0)
    if not grid_shift:
        even = (lane & 1) == 0
        outside = (even & (row == 0)) | (~even & (row == rows - 1))
        return jnp.where(outside, 0.0, jnp.where(even, before, after))
    col = row & (GRID_W - 1)
    grow = row + row0
    up = jnp.concatenate([prev_halo, x[:rows - GRID_W]], axis=0)
    down = jnp.concatenate([x[GRID_W:], next_halo], axis=0)
    m = lane & 3
    outside = (((m == 0) & (col == 0)) | ((m == 1) & (col == GRID_W - 1))
               | ((m == 2) & (grow < GRID_W)) | ((m == 3) & (grow >= t_total - GRID_W)))
    pick = jnp.where(m == 0, before, jnp.where(m == 1, after, jnp.where(m == 2, up, down)))
    return jnp.where(outside, 0.0, pick)


def _cumsum_matrices(rows):
    t = np.arange(rows)
    same = (t[:, None] // CHUNK) == (t[None, :] // CHUNK)
    tri = np.stack([same & (t[None, :] <= t[:, None]), same & (t[None, :] >= t[:, None])])
    return jnp.asarray(tri.astype(np.float32), dtype=BF16)


def _prep_kernel(*refs, grid_shift, t_total, width, cw):
    if grid_shift:
        (r_ref, k_ref, v_ref, d_ref, rp_ref, kp_ref, vp_ref, dp_ref, rn_ref, kn_ref, vn_ref, dn_ref,
         mu_ref, mud_ref, kk_ref, wcat_ref, bias_ref, seg_ref, tri_ref,
         ro_ref, ko_ref, vo_ref, kko_ref, lwf_ref, lwb_ref, af_ref, ab_ref) = refs
    else:
        (r_ref, k_ref, v_ref, d_ref, mu_ref, mud_ref, kk_ref, wcat_ref, bias_ref, seg_ref, tri_ref,
         ro_ref, ko_ref, vo_ref, kko_ref, lwf_ref, lwb_ref, af_ref, ab_ref) = refs
        rp_ref = kp_ref = vp_ref = dp_ref = rn_ref = kn_ref = vn_ref = dn_ref = None
    rows = r_ref.shape[1]
    row0 = pl.program_id(1) * rows

    def lerp(x_ref, p_ref, n_ref, mu, sl):
        x = x_ref[0, :, sl]
        ph = p_ref[0, :, sl] if grid_shift else None
        nh = n_ref[0, :, sl] if grid_shift else None
        s = _shift(x, ph, nh, t_total, row0, grid_shift)
        return x + mu * (s - x)

    dsl = slice(0, LANES)
    d = lerp(d_ref, dp_ref, dn_ref, mud_ref[...], dsl)
    lane = lax.broadcasted_iota(jnp.int32, d.shape, 1)
    d = jnp.where(lane < HEAD, jnp.tanh(d), d).astype(BF16)
    outs = (lwf_ref, lwb_ref, af_ref, ab_ref)
    for j in range(width // cw):
        sl = slice(j * cw, (j + 1) * cw)
        ro_ref[0, :, sl] = lerp(r_ref, rp_ref, rn_ref, mu_ref[0:1, sl], sl)
        vo_ref[0, :, sl] = lerp(v_ref, vp_ref, vn_ref, mu_ref[2:3, sl], sl)
        k = lerp(k_ref, kp_ref, kn_ref, mu_ref[1:2, sl], sl)
        ko_ref[0, :, sl] = k
        kk = k * kk_ref[0:1, sl]
        ss = _dot_exact_rhs(kk * kk, seg_ref[...])
        kko_ref[0, :, sl] = kk / jnp.maximum(jnp.sqrt(ss), 1e-12)
        for z in range(4):
            wsl = slice(z * width + j * cw, z * width + (j + 1) * cw)
            pre = jnp.dot(d, wcat_ref[:, wsl], preferred_element_type=F32) + bias_ref[z:z + 1, sl]
            sg = jax.nn.sigmoid(pre)
            if z < 2:
                hi, lo = _split2(sg * (-math.exp(-0.5)))
                outs[z][0, :, sl] = (jnp.dot(tri_ref[z], hi, preferred_element_type=F32)
                                     + jnp.dot(tri_ref[z], lo, preferred_element_type=F32))
            else:
                outs[z][0, :, sl] = sg


def _rwkv_prep(u, mu_shift, k_k, w_up, a_up, w0, a0, grid_shift, tt):
    b, t, _ = u.shape
    width = k_k.shape[0]
    cw = MXU_DIM
    mu = mu_shift[:3 * width].reshape(3, width)
    mud = mu_shift[3 * width:].reshape(1, LANES)
    zpad = jnp.zeros((HEAD, width), F32)
    wcat = jnp.concatenate([jnp.concatenate([w_up[0], zpad], 0), jnp.concatenate([w_up[1], zpad], 0),
                            jnp.concatenate([zpad, a_up[0]], 0), jnp.concatenate([zpad, a_up[1]], 0)],
                           axis=1).astype(BF16)
    bias = jnp.concatenate([w0, a0], axis=0)
    seg = _seg_matrix(cw)
    d_blk = (u.shape[2] - LANES) // LANES
    main = [pl.BlockSpec((1, tt, width), lambda bi, i, c=c: (bi, i, c)) for c in (1, 2, 3)]
    main.append(pl.BlockSpec((1, tt, LANES), lambda bi, i: (bi, i, d_blk)))
    ins = [u, u, u, u]
    specs = list(main)
    if grid_shift:
        hb = tt // GRID_W
        last = t // GRID_W - 1
        for off in (-1, hb):
            def imap(bi, i, c, off=off):
                return (bi, jnp.clip(i * hb + off, 0, last), c)
            specs += [pl.BlockSpec((1, GRID_W, width), functools.partial(imap, c=c)) for c in (1, 2, 3)]
            specs.append(pl.BlockSpec((1, GRID_W, LANES), functools.partial(imap, c=d_blk)))
            ins += [u, u, u, u]
    const = lambda shape: pl.BlockSpec(shape, lambda bi, i: (0, 0))
    specs += [const((3, width)), const((1, LANES)), const((1, width)), const((LANES, 4 * width)),
              const((4, width)), const((cw, cw)), pl.BlockSpec((2, tt, tt), lambda bi, i: (0, 0, 0))]
    ins += [mu, mud, k_k.reshape(1, width), wcat, bias, seg, _cumsum_matrices(tt)]
    out_spec = pl.BlockSpec((1, tt, width), lambda bi, i: (bi, i, 0))
    kern = functools.partial(_prep_kernel, grid_shift=grid_shift, t_total=t, width=width, cw=cw)
    return pl.pallas_call(
        kern,
        out_shape=[jax.ShapeDtypeStruct((b, t, width), F32)] * 8,
        grid=(b, t // tt),
        in_specs=specs,
        out_specs=[out_spec] * 8,
        compiler_params=_cparams(("arbitrary", "arbitrary")),
        name="rwkv_prep",
    )(*ins)


def _seg_matrix(n):
    i = np.arange(n) // HEAD
    return jnp.asarray((i[:, None] == i[None, :]).astype(np.float32), dtype=BF16)


def _heads_stacked(x, m0):
    return jnp.concatenate([jnp.where(m0, x, 0.0), jnp.where(m0, 0.0, x)], axis=0)


def _scan_chain(in_refs, ls, k_a, reverse, st_ref, st_idx, y_ref):
    r_ref, k_ref, v_ref, kk_ref, cl_ref, a_ref = in_refs
    n = TBLK
    cl = cl_ref[0, :, ls]
    pos = lax.broadcasted_iota(jnp.int32, (n, LANES), 0) & (CHUNK - 1)
    if reverse:
        cl_ex = jnp.where(pos == CHUNK - 1, 0.0, pltpu.roll(cl, n - 1, 0))
    else:
        cl_ex = jnp.where(pos == 0, 0.0, pltpu.roll(cl, 1, 0))
    n_chunks = n // CHUNK
    ends = [c * CHUNK if reverse else (c + 1) * CHUNK - 1 for c in range(n_chunks)]
    tot_rows = [cl[e:e + 1] for e in ends]
    tot = jnp.concatenate([jnp.broadcast_to(tr, (CHUNK, LANES)) for tr in tot_rows], axis=0)
    r, k, v, kk, a = r_ref[0, :, ls], k_ref[0, :, ls], v_ref[0, :, ls], kk_ref[0, :, ls], a_ref[0, :, ls]
    kd = k * (1.0 + (a - 1.0) * k_a)
    b = kk * a
    at = -kk * jnp.exp(cl_ex)
    rt = r * jnp.exp(cl)
    e_neg = jnp.exp(-cl)
    e_tail = jnp.exp(tot - cl)
    bh = b * e_tail
    kh = kd * e_tail
    w_chunk = [jnp.exp(tr) for tr in tot_rows]

    m0 = lax.broadcasted_iota(jnp.int32, (n, LANES), 1) < HEAD
    lhs = jnp.concatenate([_heads_stacked(at, m0), _heads_stacked(rt, m0)], axis=0)
    rhs = jnp.concatenate([b * e_neg, kd * e_neg], axis=0)
    g = _bdot_nt(lhs, rhs)
    yield

    row = lax.broadcasted_iota(jnp.int32, (n, 2 * n), 0)
    col = lax.broadcasted_iota(jnp.int32, (n, 2 * n), 1) & (n - 1)
    same = (row // CHUNK) == (col // CHUNK)
    if reverse:
        strict, incl = same & (col > row), same & (col >= row)
    else:
        strict, incl = same & (col < row), same & (col <= row)
    pick = lambda r0, cs: jnp.concatenate([g[r0:r0 + n, cs], g[r0 + n:r0 + 2 * n, cs]], axis=1)
    left, right = slice(0, n), slice(n, 2 * n)
    a_ab = jnp.where(strict, pick(0, left), 0.0)
    a_kr = jnp.concatenate([jnp.where(strict, pick(0, right), 0.0),
                            jnp.where(incl, pick(2 * n, right), 0.0)], axis=0).astype(BF16)
    a_rb = jnp.where(incl, pick(2 * n, left), 0.0).astype(BF16)

    blocks = 2 * n_chunks
    brow = lax.broadcasted_iota(jnp.int32, (2 * n, 2 * n), 0)
    bcol = lax.broadcasted_iota(jnp.int32, (2 * n, 2 * n), 1)
    on_block = (brow // CHUNK) == (bcol // CHUNK)
    spread = lambda x: jnp.where(on_block, jnp.concatenate([x.astype(BF16)] * blocks, axis=0), 0.0)
    steps = int(math.log2(CHUNK)) - 1
    q = a_ab[:CHUNK]
    for c in range(1, n_chunks):
        q = q + a_ab[c * CHUNK:(c + 1) * CHUNK]
    p = _bdot(q, spread(q))
    kr = jnp.dot(a_kr, _heads_stacked(v, m0).astype(BF16), preferred_element_type=F32)
    yield
    akv, rkv = kr[:n], kr[n:]
    for _ in range(steps - 1):
        res = _bdot(jnp.concatenate([p, q], axis=0), spread(p))
        yield
        q = q + p + res[CHUNK:]
        p = res[:CHUNK]
    fin = _bdot(q, spread(p))
    yield
    q = q + p + fin
    q16 = jnp.where(same, jnp.concatenate([q] * n_chunks, axis=0), 0.0).astype(BF16)

    qx = jnp.dot(q16, jnp.concatenate([_heads_stacked(at, m0), _heads_stacked(akv, m0)], axis=1).astype(BF16),
                 preferred_element_type=F32)
    yield
    a_hat = at + qx[:, :LANES]
    uv = akv + qx[:, LANES:]
    both = jnp.dot(a_rb, jnp.concatenate([_heads_stacked(a_hat, m0), _heads_stacked(uv, m0)],
                                         axis=1).astype(BF16), preferred_element_type=F32)
    yield
    r_hat = rt + both[:, :LANES]
    yv = both[:, LANES:] + rkv

    srow = lax.broadcasted_iota(jnp.int32, (LANES, LANES), 0)
    scol = lax.broadcasted_iota(jnp.int32, (LANES, LANES), 1)
    pair_diag = (srow < HEAD) == (scol < HEAD)
    state = st_ref[st_idx]
    for c in (range(n_chunks - 1, -1, -1) if reverse else range(n_chunks)):
        sl = slice(c * CHUNK, (c + 1) * CHUNK)
        proj = _bdot_nt(jnp.concatenate([a_hat[sl], r_hat[sl]], axis=0), state)
        yield
        u_c = proj[:CHUNK] + uv[sl]
        y_ref[0, sl, ls] = proj[CHUNK:] + yv[sl]
        upd = _bdot_tn(jnp.concatenate([u_c, v[sl]], axis=0), jnp.concatenate([bh[sl], kh[sl]], axis=0))
        yield
        state = state * w_chunk[c] + jnp.where(pair_diag, upd, 0.0)
    st_ref[st_idx] = state


def _run_interleaved(chains):
    chains = list(chains)
    while chains:
        alive = []
        for ch in chains:
            try:
                next(ch)
                alive.append(ch)
            except StopIteration:
                pass
        chains = alive


def _scan_kernel(*refs, has_init, want_state, groups):
    fwd_refs, bwd_refs = refs[:6], refs[6:12]
    ka_ref = refs[12]
    pos = 13
    s0_ref = None
    if has_init:
        s0_ref = refs[pos]
        pos += 1
    yf_ref, yb_ref = refs[pos:pos + 2]
    pos += 2
    so_ref = None
    if want_state:
        so_ref = refs[pos]
        pos += 1
    st_ref = refs[pos]
    i = pl.program_id(2)

    @pl.when(i == 0)
    def _():
        if has_init:
            st_ref[...] = s0_ref[0]
        else:
            st_ref[...] = jnp.zeros(st_ref.shape, F32)

    chains = []
    for g in range(groups):
        ls = slice(g * LANES, (g + 1) * LANES)
        k_a = ka_ref[:, ls]
        chains.append(_scan_chain(fwd_refs, ls, k_a, False, st_ref, (0, g), yf_ref))
        chains.append(_scan_chain(bwd_refs, ls, k_a, True, st_ref, (1, g), yb_ref))
    _run_interleaved(chains)

    if want_state:
        @pl.when(i == pl.num_programs(2) - 1)
        def _():
            for z in range(2):
                for g in range(groups):
                    s = st_ref[z, g]
                    so_ref[0, z, 2 * g] = s[:HEAD, :HEAD]
                    so_ref[0, z, 2 * g + 1] = s[HEAD:, HEAD:]


def _rwkv_scan(r, k, v, kk, cl_f, cl_b, a_f, a_b, k_a, s0, want_state):
    b, t, width = r.shape
    pairs = width // LANES
    g = SCAN_PAIRS
    gw = g * LANES
    nt = t // TBLK
    fwd = pl.BlockSpec((1, TBLK, gw), lambda bi, p, i: (bi, i, p))
    bwd = pl.BlockSpec((1, TBLK, gw), lambda bi, p, i: (bi, nt - 1 - i, p))
    st_spec = pl.BlockSpec((1, 2, g, LANES, LANES), lambda bi, p, i: (bi, 0, p, 0, 0))
    specs = [fwd] * 6 + [bwd] * 6 + [pl.BlockSpec((1, gw), lambda bi, p, i: (0, p))]
    ins = [r, k, v, kk, cl_f, a_f, r, k, v, kk, cl_b, a_b, k_a.reshape(1, width)]
    if s0 is not None:
        specs.append(st_spec)
        ins.append(s0)
    y_shape = jax.ShapeDtypeStruct((b, t, width), F32)
    out_shape = [y_shape, y_shape]
    out_specs = [fwd, bwd]
    if want_state:
        out_shape.append(jax.ShapeDtypeStruct((b, 2, 2 * pairs, HEAD, HEAD), F32))
        out_specs.append(pl.BlockSpec((1, 2, 2 * g, HEAD, HEAD), lambda bi, p, i: (bi, 0, p, 0, 0)))
    kern = functools.partial(_scan_kernel, has_init=s0 is not None, want_state=want_state, groups=g)
    return pl.pallas_call(
        kern,
        out_shape=out_shape,
        grid=(b, pairs // g, nt),
        in_specs=specs,
        out_specs=out_specs,
        scratch_shapes=[pltpu.VMEM((2, g, LANES, LANES), F32)],
        compiler_params=_cparams(("arbitrary", "arbitrary", "arbitrary")),
        name="rwkv7_scan",
    )(*ins)


def _pair_states(s):
    b, z, h, n, _ = s.shape
    sp = s.reshape(b, z, h // 2, 2, n, n)
    zero = jnp.zeros_like(sp[:, :, :, 0])
    top = jnp.concatenate([sp[:, :, :, 0], zero], axis=-1)
    bot = jnp.concatenate([zero, sp[:, :, :, 1]], axis=-1)
    return jnp.concatenate([top, bot], axis=-2)


def _post_kernel(yf_ref, yb_ref, r_ref, k_ref, v_ref, gf_ref, gr_ref, g0_ref, g1_ref, fo_ref, x_ref, mod_ref,
                 vec_ref, seg_ref, wf_ref, wr_ref, wo_ref, o_ref, z_ref, *, cw):
    width = yf_ref.shape[2]
    inv_n = 1.0 / HEAD
    for j in range(width // cw):
        sl = slice(j * cw, (j + 1) * cw)
        seg = seg_ref[...]
        y = yf_ref[0, :, sl] + yb_ref[0, :, sl]
        mean = _dot_exact_rhs(y, seg) * inv_n
        d = y - mean
        var = _dot_exact_rhs(d * d, seg) * inv_n
        yn = d * lax.rsqrt(var + GN_EPS) * vec_ref[0:1, sl] + vec_ref[1:2, sl]
        rk = _dot_exact_rhs(r_ref[0, :, sl] * k_ref[0, :, sl] * vec_ref[2:3, sl], seg)
        z = (yn + rk * v_ref[0, :, sl]) * _silu(gr_ref[0, :, sl])
        z_ref[:, sl] = z.astype(BF16)
    out_r = jnp.dot(z_ref[...], wr_ref[...], preferred_element_type=F32)
    zf = (fo_ref[0] * _silu(gf_ref[0])).astype(BF16)
    out_f = jnp.dot(zf, wf_ref[...], preferred_element_type=F32)
    merged = jax.nn.sigmoid(g0_ref[0]) * out_f + jax.nn.sigmoid(g1_ref[0]) * out_r
    o = jnp.dot(merged.astype(BF16), wo_ref[...], preferred_element_type=F32)
    xn = x_ref[0] + mod_ref[0][2:3] * o
    o_ref[0] = xn * lax.rsqrt(jnp.mean(xn * xn, axis=-1, keepdims=True) + RMS_EPS) * vec_ref[3:4, :]


def _post(yf, yb, r, k, v, u, fo, x, mod, lnx_g, lnx_b, r_k, final_g, w_proj_f, w_proj_r, w_out, tm):
    b, t, d = x.shape
    width = yf.shape[2]
    fw = fo.shape[2]
    cw = MXU_DIM
    vec = jnp.stack([lnx_g, lnx_b, r_k.reshape(-1), final_g], axis=0)
    tok = lambda w, c: pl.BlockSpec((1, tm, w), lambda bi, i: (bi, i, c))
    const = lambda shape: pl.BlockSpec(shape, lambda bi, i: (0,) * len(shape), pipeline_mode=pl.Buffered(1))
    gr_off = 2 * fw + 3 * width
    mg_off = gr_off + width
    assert gr_off % width == 0 and mg_off % d == 0
    gr_blk, mg_blk = gr_off // width, mg_off // d
    specs = [tok(width, 0)] * 5 + [tok(fw, 1), tok(width, gr_blk), tok(d, mg_blk), tok(d, mg_blk + 1),
                                   tok(fw, 0), tok(d, 0), pl.BlockSpec((1, SUBLANES, d), lambda bi, i: (bi, 0, 0)),
                                   const((4, d)), const((cw, cw)), const((fw, d)), const((width, d)),
                                   const((d, d))]
    return pl.pallas_call(
        functools.partial(_post_kernel, cw=cw),
        out_shape=jax.ShapeDtypeStruct((b, t, d), F32),
        grid=(b, t // tm),
        in_specs=specs,
        out_specs=tok(d, 0),
        scratch_shapes=[pltpu.VMEM((tm, width), BF16)],
        compiler_params=_cparams(("arbitrary", "arbitrary")),
        name="post_mix",
    )(yf, yb, r, k, v, u, u, u, u, fo, x, mod, vec, _seg_matrix(cw),
      w_proj_f.astype(BF16), w_proj_r.astype(BF16), w_out.astype(BF16))


def _mixer_path(x, mod, s0, want_state, grid_shift, p, tiles, shared_mod):
    b, t, d = x.shape
    tokens = (lambda a: a.reshape(1, b * t, a.shape[2])) if shared_mod else (lambda a: a)
    u_tok = _inproj(tokens(x), mod, p["norm_g"], p["w_in"], tiles["inproj"], p["w_in_segments"])
    u = u_tok.reshape(b, t, u_tok.shape[2])
    fo = _fourier(u, t, p["fourier_width"], tiles["fourier_n1"], tiles["fourier"])
    r, k, v, kk, cl_f, cl_b, a_f, a_b = _rwkv_prep(
        u, p["mu_shift"], p["k_k"], p["w_up"], p["a_up"], p["w0"], p["a0"], grid_shift, tiles["prep"])
    res = _rwkv_scan(r, k, v, kk, cl_f, cl_b, a_f, a_b, p["k_a"], s0, want_state)
    y = _post(tokens(res[0]), tokens(res[1]), tokens(r), tokens(k), tokens(v), u_tok, tokens(fo), tokens(x), mod,
              p["lnx_g"], p["lnx_b"], p["r_k"], p["final_g"], p["w_proj_f"], p["w_proj_r"], p["w_out"],
              tiles["post"])
    return y.reshape(b, t, d), (res[2] if want_state else None)


def _tiles(t, grid_shift):
    token_tile = 2 * MXU_DIM
    if not grid_shift:
        return dict(inproj=token_tile, fourier=t, fourier_n1=1, prep=t, post=token_tile)
    return dict(inproj=token_tile, fourier=MXU_DIM, fourier_n1=4, prep=token_tile, post=token_tile)


def kernel(x_prompt, x_sample, state_rwkv, c, c_ctx, norm_g, w_ada, b_ada, w_in, mu_shift, w0, w_up, a0,
           a_up, k_k, k_a, r_k, lnx_g, lnx_b, w_proj_f, w_proj_r, w_out, final_g):
    depth = w_in.shape[0]
    assert depth == 1, "the final norm is fused into the single layer's post kernel"
    bp, tp, d = x_prompt.shape
    bs, ts, _ = x_sample.shape
    width = k_k.shape[1]
    fw = w_proj_f.shape[1]
    l = 0
    sh_end = 2 * fw + 3 * width
    rank2 = mu_shift.shape[1] - 3 * width
    n_in = w_in.shape[2]
    segments = ((0, 0, sh_end), (sh_end + rank2, sh_end, n_in - sh_end - rank2), (sh_end, n_in - rank2, rank2))
    p = dict(norm_g=norm_g[l], w_in=w_in[l].astype(BF16), w_in_segments=segments, fourier_width=fw,
             mu_shift=mu_shift[l], k_k=k_k[l], w_up=w_up[l], a_up=a_up[l], w0=w0[l], a0=a0[l], k_a=k_a[l],
             lnx_g=lnx_g[l], lnx_b=lnx_b[l], r_k=r_k[l], final_g=final_g, w_proj_f=w_proj_f[l],
             w_proj_r=w_proj_r[l], w_out=w_out[l])

    assert 1 + bs <= SUBLANES
    cvec = jnp.concatenate([c_ctx[None], c, jnp.zeros((SUBLANES - 1 - bs, d), F32)], axis=0)
    m3 = _modulation(cvec, w_ada[l], b_ada[l]).reshape(SUBLANES, 3, d)
    mod = jnp.concatenate([m3, jnp.zeros((SUBLANES, SUBLANES - 3, d), F32)], axis=1)

    yp, sp = _mixer_path(x_prompt, mod[0:1], None, True, False, p, _tiles(tp, False), shared_mod=True)
    s0 = _pair_states(state_rwkv[:, l])
    ys, _ = _mixer_path(x_sample, mod[1:1 + bs], s0, False, True, p, _tiles(ts, True), shared_mod=False)
    new_state = sp[:, None]
    return yp, ys, new_state
```

```python
import functools
import math

import numpy as np
import jax
import jax.numpy as jnp
from jax import lax
from jax.experimental import pallas as pl
from jax.experimental.pallas import tpu as pltpu

F32 = jnp.float32
BF16 = jnp.bfloat16

SUBLANES = 8
LANES = 128
MXU_DIM = 256
VMEM_LIMIT = 56 * 1024 * 1024

HEAD = 64
GRID_W = 64
FOURIER_GROUP = 128
RMS_EPS = 1e-6
GN_EPS = 64e-5

TBLK = 128
CHUNK = 64
SCAN_PAIRS = 8


def _cparams(sem):
    return pltpu.CompilerParams(dimension_semantics=sem, vmem_limit_bytes=VMEM_LIMIT)


def _bdot(a, b):
    return jnp.dot(a.astype(BF16), b.astype(BF16), preferred_element_type=F32)


def _bdot_nt(a, b):
    return lax.dot_general(a.astype(BF16), b.astype(BF16), (((1,), (1,)), ((), ())),
                           preferred_element_type=F32)


def _bdot_tn(a, b):
    return lax.dot_general(a.astype(BF16), b.astype(BF16), (((0,), (0,)), ((), ())),
                           preferred_element_type=F32)


def _split2(x):
    hi = x.astype(BF16)
    lo = (x - hi.astype(F32)).astype(BF16)
    return hi, lo


def _dot_exact_rhs(x, m):
    hi, lo = _split2(x)
    return (jnp.dot(hi, m, preferred_element_type=F32) + jnp.dot(lo, m, preferred_element_type=F32))


def _silu(x):
    return x * jax.nn.sigmoid(x)


def _mod_kernel(c_ref, w_ref, b_ref, o_ref):
    c = c_ref[...]
    o_ref[...] = jnp.dot(_silu(c), w_ref[...], precision=lax.Precision.HIGHEST,
                         preferred_element_type=F32) + b_ref[...]


def _modulation(cvec, w_ada, b_ada):
    rows, d = cvec.shape
    n = w_ada.shape[1]
    tn = n // 2
    return pl.pallas_call(
        _mod_kernel,
        out_shape=jax.ShapeDtypeStruct((rows, n), F32),
        grid=(n // tn,),
        in_specs=[pl.BlockSpec((rows, d), lambda j: (0, 0)),
                  pl.BlockSpec((d, tn), lambda j: (0, j)),
                  pl.BlockSpec((1, tn), lambda j: (0, j))],
        out_specs=pl.BlockSpec((rows, tn), lambda j: (0, j)),
        compiler_params=_cparams(("arbitrary",)),
        name="adaln_modulation",
    )(cvec, w_ada, b_ada.reshape(1, n))


def _inproj_kernel(x_ref, mod_ref, g_ref, w_ref, o_ref, *, tn, segments):
    x = x_ref[0]
    y = x * lax.rsqrt(jnp.mean(x * x, axis=-1, keepdims=True) + RMS_EPS) * g_ref[...]
    m = mod_ref[0]
    h = (y * (1.0 + m[1:2]) + m[0:1]).astype(BF16)
    for src, dst, size in segments:
        for j0 in range(0, size, tn):
            w = min(tn, size - j0)
            o_ref[0, :, dst + j0:dst + j0 + w] = jnp.dot(h, w_ref[:, src + j0:src + j0 + w],
                                                         preferred_element_type=F32)


def _inproj(x, mod, norm_g, w_bf16, tm, segments):
    b, t, d = x.shape
    n = w_bf16.shape[1]
    return pl.pallas_call(
        functools.partial(_inproj_kernel, tn=2 * MXU_DIM, segments=segments),
        out_shape=jax.ShapeDtypeStruct((b, t, n), F32),
        grid=(b, t // tm),
        in_specs=[pl.BlockSpec((1, tm, d), lambda bi, i: (bi, i, 0)),
                  pl.BlockSpec((1, SUBLANES, d), lambda bi, i: (bi, 0, 0)),
                  pl.BlockSpec((1, d), lambda bi, i: (0, 0)),
                  pl.BlockSpec((d, n), lambda bi, i: (0, 0), pipeline_mode=pl.Buffered(1))],
        out_specs=pl.BlockSpec((1, tm, n), lambda bi, i: (bi, i, 0)),
        compiler_params=_cparams(("arbitrary", "arbitrary")),
        name="norm_mod_inproj",
    )(x, mod, norm_g.reshape(1, d), w_bf16)


_ROT = ((1.0, 0, -1.0, 1), (-1.0, 1, -1.0, 0), (-1.0, 0, 1.0, 1), (1.0, 1, 1.0, 0))


def _fourier_kernel(*refs, n1, t2, rows):
    if n1 > 1:
        xf_ref, cc_ref, er_ref, ei_ref, twr_ref, twi_ref, o_ref, g_ref, il_ref = refs
    else:
        xf_ref, cc_ref, er_ref, ei_ref, o_ref, g_ref, il_ref = refs
    width = xf_ref.shape[2]

    @pl.when(pl.program_id(1) == 0)
    def _():
        def body(i, carry):
            r0 = pl.multiple_of(i * rows, rows)
            pq = []
            for a in range(n1):
                x = xf_ref[0, pl.ds(a * t2 + r0, rows), :]
                parts = [_bdot(x[:, g * FOURIER_GROUP:(g + 1) * FOURIER_GROUP], cc_ref[...])
                         for g in range(width // FOURIER_GROUP)]
                pq.append((jnp.concatenate([p[:, :FOURIER_GROUP] for p in parts], axis=1),
                           jnp.concatenate([p[:, FOURIER_GROUP:] for p in parts], axis=1)))
            for f1 in range(n1):
                re = im = None
                for a in range(n1):
                    sr, cr, si, ci = _ROT[(a * f1) % 4]
                    tr, ti = sr * pq[a][cr], si * pq[a][ci]
                    re = tr if re is None else re + tr
                    im = ti if im is None else im + ti
                if f1 > 0:
                    lane_groups = width // LANES
                    c = jnp.concatenate([twr_ref[f1, pl.ds(r0, rows), :]] * lane_groups, axis=1)
                    s = jnp.concatenate([twi_ref[f1, pl.ds(r0, rows), :]] * lane_groups, axis=1)
                    re, im = re * c - im * s, re * s + im * c
                g_ref[f1, pl.ds(r0, rows), :] = re.astype(BF16)
                g_ref[f1, pl.ds(t2 + r0, rows), :] = im.astype(BF16)
            return carry
        lax.fori_loop(0, t2 // rows, body, 0)

    tf = er_ref.shape[1]
    for f1 in range(n1):
        res = (jnp.dot(er_ref[0], g_ref[f1, 0:t2], preferred_element_type=F32)
               + jnp.dot(ei_ref[0], g_ref[f1, t2:2 * t2], preferred_element_type=F32))
        if n1 == 1:
            o_ref[0] = res
        else:
            for g in range(width // LANES):
                il_ref[g, pl.ds(f1, tf, stride=n1), :] = res[:, g * LANES:(g + 1) * LANES]
    if n1 > 1:
        for g in range(width // LANES):
            o_ref[0, :, g * LANES:(g + 1) * LANES] = il_ref[g]


def _dft_consts(t, n1):
    t2 = t // n1
    g = FOURIER_GROUP
    j = np.arange(g)
    ang = 2.0 * np.pi * ((j[:, None] * j[None, :]) % g) / g
    cc = jnp.asarray(np.concatenate([np.cos(ang), np.sin(ang)], axis=1) / math.sqrt(g), dtype=F32).astype(BF16)
    scale = 1.0 / math.sqrt(t)
    if n1 == 1:
        k = np.arange(t)
        ang = 2.0 * np.pi * ((k[:, None] * k[None, :]) % t) / t
        tab = lambda m: jnp.asarray(m[None] * scale, dtype=F32).astype(BF16)
        return tab(np.cos(ang)), tab(np.sin(ang)), cc, None
    sa = 1 << (int(math.log2(t2)) // 2)
    sb = t2 // sa
    assert sa * sb == t2 and n1 * t2 == t
    tt = np.arange(t2)
    bb = np.arange(sb)[:, None]
    ang_x = 2.0 * np.pi * ((tt[None, :] * bb) % t2) / t2
    aa = np.arange(sa)[:, None]
    ang_y = 2.0 * np.pi * ((tt[None, :] * aa) % sa) / sa
    xr, xi = jnp.asarray(np.cos(ang_x) * scale, F32), jnp.asarray(-np.sin(ang_x) * scale, F32)
    yr, yi = jnp.asarray(np.cos(ang_y), F32), jnp.asarray(-np.sin(ang_y), F32)
    xr, xi = xr[None], xi[None]
    yr, yi = yr[:, None], yi[:, None]
    re = (xr * yr - xi * yi).reshape(1, t2, t2)
    im = (xr * yi + xi * yr).reshape(1, t2, t2)
    ang_w = 2.0 * np.pi * (tt[None, :] * np.arange(n1)[:, None]) / t
    rep = lambda m: jnp.broadcast_to(jnp.asarray(m, F32)[:, :, None], (n1, t2, LANES))
    return re.astype(BF16), (-im).astype(BF16), cc, (rep(np.cos(ang_w)), rep(-np.sin(ang_w)))


def _fourier(u, t, width, n1, tf):
    b = u.shape[0]
    t2 = t // n1
    e_re, e_nim, cc, twiddles = _dft_consts(t, n1)
    e_spec = pl.BlockSpec((1, tf, t2), lambda bi, i: (0, i, 0))
    specs = [pl.BlockSpec((1, t, width), lambda bi, i: (bi, 0, 0)),
             pl.BlockSpec((FOURIER_GROUP, 2 * FOURIER_GROUP), lambda bi, i: (0, 0)), e_spec, e_spec]
    ins = [u, cc, e_re, e_nim]
    if twiddles is not None:
        specs += [pl.BlockSpec((n1, t2, LANES), lambda bi, i: (0, 0, 0))] * 2
        ins += list(twiddles)
    kern = functools.partial(_fourier_kernel, n1=n1, t2=t2, rows=min(t2, MXU_DIM))
    return pl.pallas_call(
        kern,
        out_shape=jax.ShapeDtypeStruct((b, t, width), F32),
        grid=(b, t2 // tf),
        in_specs=specs,
        out_specs=pl.BlockSpec((1, n1 * tf, width), lambda bi, i: (bi, i, 0)),
        scratch_shapes=[pltpu.VMEM((n1, 2 * t2, width), BF16),
                        pltpu.VMEM((width // LANES, n1 * tf if n1 > 1 else SUBLANES, LANES), F32)],
        compiler_params=_cparams(("arbitrary", "arbitrary")),
        name="fourier_mix",
    )(*ins)


def _shift(x, prev_halo, next_halo, t_total, row0, grid_shift):
    rows, lanes = x.shape
    row = lax.broadcasted_iota(jnp.int32, (rows, lanes), 0)
    lane = lax.broadcasted_iota(jnp.int32, (rows, lanes), 1)
    before = pltpu.roll(x, 1, 0)
    after = pltpu.roll(x, rows - 1, 0)
    if not grid_shift:
        even = (lane & 1) == 0
        outside = (even & (row == 0)) | (~even & (row == rows - 1))
        return jnp.where(outside, 0.0, jnp.where(even, before, after))
    col = row & (GRID_W - 1)
    grow = row + row0
    up = jnp.concatenate([prev_halo, x[:rows - GRID_W]], axis=0)
    down = jnp.concatenate([x[GRID_W:], next_halo], axis=0)
    m = lane & 3
    outside = (((m == 0) & (col == 0)) | ((m == 1) & (col == GRID_W - 1))
               | ((m == 2) & (grow < GRID_W)) | ((m == 3) & (grow >= t_total - GRID_W)))
    pick = jnp.where(m == 0, before, jnp.where(m == 1, after, jnp.where(m == 2, up, down)))
    return jnp.where(outside, 0.0, pick)


def _cumsum_matrices(rows):
    t = np.arange(rows)
    same = (t[:, None] // CHUNK) == (t[None, :] // CHUNK)
    tri = np.stack([same & (t[None, :] <= t[:, None]), same & (t[None, :] >= t[:, None])])
    return jnp.asarray(tri.astype(np.float32), dtype=BF16)


def _prep_kernel(*refs, grid_shift, t_total, width, cw):
    if grid_shift:
        (r_ref, k_ref, v_ref, d_ref, rp_ref, kp_ref, vp_ref, dp_ref, rn_ref, kn_ref, vn_ref, dn_ref,
         mu_ref, mud_ref, kk_ref, wcat_ref, bias_ref, seg_ref, tri_ref,
         ro_ref, ko_ref, vo_ref, kko_ref, lwf_ref, lwb_ref, af_ref, ab_ref) = refs
    else:
        (r_ref, k_ref, v_ref, d_ref, mu_ref, mud_ref, kk_ref, wcat_ref, bias_ref, seg_ref, tri_ref,
         ro_ref, ko_ref, vo_ref, kko_ref, lwf_ref, lwb_ref, af_ref, ab_ref) = refs
        rp_ref = kp_ref = vp_ref = dp_ref = rn_ref = kn_ref = vn_ref = dn_ref = None
    rows = r_ref.shape[1]
    row0 = pl.program_id(1) * rows

    def lerp(x_ref, p_ref, n_ref, mu, sl):
        x = x_ref[0, :, sl]
        ph = p_ref[0, :, sl] if grid_shift else None
        nh = n_ref[0, :, sl] if grid_shift else None
        s = _shift(x, ph, nh, t_total, row0, grid_shift)
        return x + mu * (s - x)

    dsl = slice(0, LANES)
    d = lerp(d_ref, dp_ref, dn_ref, mud_ref[...], dsl)
    lane = lax.broadcasted_iota(jnp.int32, d.shape, 1)
    d = jnp.where(lane < HEAD, jnp.tanh(d), d).astype(BF16)
    outs = (lwf_ref, lwb_ref, af_ref, ab_ref)
    for j in range(width // cw):
        sl = slice(j * cw, (j + 1) * cw)
        ro_ref[0, :, sl] = lerp(r_ref, rp_ref, rn_ref, mu_ref[0:1, sl], sl)
        vo_ref[0, :, sl] = lerp(v_ref, vp_ref, vn_ref, mu_ref[2:3, sl], sl)
        k = lerp(k_ref, kp_ref, kn_ref, mu_ref[1:2, sl], sl)
        ko_ref[0, :, sl] = k
        kk = k * kk_ref[0:1, sl]
        ss = _dot_exact_rhs(kk * kk, seg_ref[...])
        kko_ref[0, :, sl] = kk / jnp.maximum(jnp.sqrt(ss), 1e-12)
        for z in range(4):
            wsl = slice(z * width + j * cw, z * width + (j + 1) * cw)
            pre = jnp.dot(d, wcat_ref[:, wsl], preferred_element_type=F32) + bias_ref[z:z + 1, sl]
            sg = jax.nn.sigmoid(pre)
            if z < 2:
                hi, lo = _split2(sg * (-math.exp(-0.5)))
                outs[z][0, :, sl] = (jnp.dot(tri_ref[z], hi, preferred_element_type=F32)
                                     + jnp.dot(tri_ref[z], lo, preferred_element_type=F32))
            else:
                outs[z][0, :, sl] = sg


def _rwkv_prep(u, mu_shift, k_k, w_up, a_up, w0, a0, grid_shift, tt):
    b, t, _ = u.shape
    width = k_k.shape[0]
    cw = MXU_DIM
    mu = mu_shift[:3 * width].reshape(3, width)
    mud = mu_shift[3 * width:].reshape(1, LANES)
    zpad = jnp.zeros((HEAD, width), F32)
    wcat = jnp.concatenate([jnp.concatenate([w_up[0], zpad], 0), jnp.concatenate([w_up[1], zpad], 0),
                            jnp.concatenate([zpad, a_up[0]], 0), jnp.concatenate([zpad, a_up[1]], 0)],
                           axis=1).astype(BF16)
    bias = jnp.concatenate([w0, a0], axis=0)
    seg = _seg_matrix(cw)
    d_blk = (u.shape[2] - LANES) // LANES
    main = [pl.BlockSpec((1, tt, width), lambda bi, i, c=c: (bi, i, c)) for c in (1, 2, 3)]
    main.append(pl.BlockSpec((1, tt, LANES), lambda bi, i: (bi, i, d_blk)))
    ins = [u, u, u, u]
    specs = list(main)
    if grid_shift:
        hb = tt // GRID_W
        last = t // GRID_W - 1
        for off in (-1, hb):
            def imap(bi, i, c, off=off):
                return (bi, jnp.clip(i * hb + off, 0, last), c)
            specs += [pl.BlockSpec((1, GRID_W, width), functools.partial(imap, c=c)) for c in (1, 2, 3)]
            specs.append(pl.BlockSpec((1, GRID_W, LANES), functools.partial(imap, c=d_blk)))
            ins += [u, u, u, u]
    const = lambda shape: pl.BlockSpec(shape, lambda bi, i: (0, 0))
    specs += [const((3, width)), const((1, LANES)), const((1, width)), const((LANES, 4 * width)),
              const((4, width)), const((cw, cw)), pl.BlockSpec((2, tt, tt), lambda bi, i: (0, 0, 0))]
    ins += [mu, mud, k_k.reshape(1, width), wcat, bias, seg, _cumsum_matrices(tt)]
    out_spec = pl.BlockSpec((1, tt, width), lambda bi, i: (bi, i, 0))
    kern = functools.partial(_prep_kernel, grid_shift=grid_shift, t_total=t, width=width, cw=cw)
    return pl.pallas_call(
        kern,
        out_shape=[jax.ShapeDtypeStruct((b, t, width), F32)] * 8,
        grid=(b, t // tt),
        in_specs=specs,
        out_specs=[out_spec] * 8,
        compiler_params=_cparams(("arbitrary", "arbitrary")),
        name="rwkv_prep",
    )(*ins)


def _seg_matrix(n):
    i = np.arange(n) // HEAD
    return jnp.asarray((i[:, None] == i[None, :]).astype(np.float32), dtype=BF16)


def _heads_stacked(x, m0):
    return jnp.concatenate([jnp.where(m0, x, 0.0), jnp.where(m0, 0.0, x)], axis=0)


def _scan_chain(in_refs, ls, k_a, reverse, st_ref, st_idx, y_ref):
    r_ref, k_ref, v_ref, kk_ref, cl_ref, a_ref = in_refs
    n = TBLK
    cl = cl_ref[0, :, ls]
    pos = lax.broadcasted_iota(jnp.int32, (n, LANES), 0) & (CHUNK - 1)
    if reverse:
        cl_ex = jnp.where(pos == CHUNK - 1, 0.0, pltpu.roll(cl, n - 1, 0))
    else:
        cl_ex = jnp.where(pos == 0, 0.0, pltpu.roll(cl, 1, 0))
    n_chunks = n // CHUNK
    ends = [c * CHUNK if reverse else (c + 1) * CHUNK - 1 for c in range(n_chunks)]
    tot_rows = [cl[e:e + 1] for e in ends]
    tot = jnp.concatenate([jnp.broadcast_to(tr, (CHUNK, LANES)) for tr in tot_rows], axis=0)
    r, k, v, kk, a = r_ref[0, :, ls], k_ref[0, :, ls], v_ref[0, :, ls], kk_ref[0, :, ls], a_ref[0, :, ls]
    kd = k * (1.0 + (a - 1.0) * k_a)
    b = kk * a
    at = -kk * jnp.exp(cl_ex)
    rt = r * jnp.exp(cl)
    e_neg = jnp.exp(-cl)
    e_tail = jnp.exp(tot - cl)
    bh = b * e_tail
    kh = kd * e_tail
    w_chunk = [jnp.exp(tr) for tr in tot_rows]

    m0 = lax.broadcasted_iota(jnp.int32, (n, LANES), 1) < HEAD
    lhs = jnp.concatenate([_heads_stacked(at, m0), _heads_stacked(rt, m0)], axis=0)
    rhs = jnp.concatenate([b * e_neg, kd * e_neg], axis=0)
    g = _bdot_nt(lhs, rhs)
    yield

    row = lax.broadcasted_iota(jnp.int32, (n, 2 * n), 0)
    col = lax.broadcasted_iota(jnp.int32, (n, 2 * n), 1) & (n - 1)
    same = (row // CHUNK) == (col // CHUNK)
    if reverse:
        strict, incl = same & (col > row), same & (col >= row)
    else:
        strict, incl = same & (col < row), same & (col <= row)
    pick = lambda r0, cs: jnp.concatenate([g[r0:r0 + n, cs], g[r0 + n:r0 + 2 * n, cs]], axis=1)
    left, right = slice(0, n), slice(n, 2 * n)
    a_ab = jnp.where(strict, pick(0, left), 0.0)
    a_kr = jnp.concatenate([jnp.where(strict, pick(0, right), 0.0),
                            jnp.where(incl, pick(2 * n, right), 0.0)], axis=0).astype(BF16)
    a_rb = jnp.where(incl, pick(2 * n, left), 0.0).astype(BF16)

    blocks = 2 * n_chunks
    brow = lax.broadcasted_iota(jnp.int32, (2 * n, 2 * n), 0)
    bcol = lax.broadcasted_iota(jnp.int32, (2 * n, 2 * n), 1)
    on_block = (brow // CHUNK) == (bcol // CHUNK)
    spread = lambda x: jnp.where(on_block, jnp.concatenate([x.astype(BF16)] * blocks, axis=0), 0.0)
    steps = int(math.log2(CHUNK)) - 1
    q = a_ab[:CHUNK]
    for c in range(1, n_chunks):
        q = q + a_ab[c * CHUNK:(c + 1) * CHUNK]
    p = _bdot(q, spread(q))
    kr = jnp.dot(a_kr, _heads_stacked(v, m0).astype(BF16), preferred_element_type=F32)
    yield
    akv, rkv = kr[:n], kr[n:]
    for _ in range(steps - 1):
        res = _bdot(jnp.concatenate([p, q], axis=0), spread(p))
        yield
        q = q + p + res[CHUNK:]
        p = res[:CHUNK]
    fin = _bdot(q, spread(p))
    yield
    q = q + p + fin
    q16 = jnp.where(same, jnp.concatenate([q] * n_chunks, axis=0), 0.0).astype(BF16)

    qx = jnp.dot(q16, jnp.concatenate([_heads_stacked(at, m0), _heads_stacked(akv, m0)], axis=1).astype(BF16),
                 preferred_element_type=F32)
    yield
    a_hat = at + qx[:, :LANES]
    uv = akv + qx[:, LANES:]
    both = jnp.dot(a_rb, jnp.concatenate([_heads_stacked(a_hat, m0), _heads_stacked(uv, m0)],
                                         axis=1).astype(BF16), preferred_element_type=F32)
    yield
    r_hat = rt + both[:, :LANES]
    yv = both[:, LANES:] + rkv

    srow = lax.broadcasted_iota(jnp.int32, (LANES, LANES), 0)
    scol = lax.broadcasted_iota(jnp.int32, (LANES, LANES), 1)
    pair_diag = (srow < HEAD) == (scol < HEAD)
    state = st_ref[st_idx]
    for c in (range(n_chunks - 1, -1, -1) if reverse else range(n_chunks)):
        sl = slice(c * CHUNK, (c + 1) * CHUNK)
        proj = _bdot_nt(jnp.concatenate([a_hat[sl], r_hat[sl]], axis=0), state)
        yield
        u_c = proj[:CHUNK] + uv[sl]
        y_ref[0, sl, ls] = proj[CHUNK:] + yv[sl]
        upd = _bdot_tn(jnp.concatenate([u_c, v[sl]], axis=0), jnp.concatenate([bh[sl], kh[sl]], axis=0))
        yield
        state = state * w_chunk[c] + jnp.where(pair_diag, upd, 0.0)
    st_ref[st_idx] = state


def _run_interleaved(chains):
    chains = list(chains)
    while chains:
        alive = []
        for ch in chains:
            try:
                next(ch)
                alive.append(ch)
            except StopIteration:
                pass
        chains = alive


def _scan_kernel(*refs, has_init, want_state, groups):
    fwd_refs, bwd_refs = refs[:6], refs[6:12]
    ka_ref = refs[12]
    pos = 13
    s0_ref = None
    if has_init:
        s0_ref = refs[pos]
        pos += 1
    yf_ref, yb_ref = refs[pos:pos + 2]
    pos += 2
    so_ref = None
    if want_state:
        so_ref = refs[pos]
        pos += 1
    st_ref = refs[pos]
    i = pl.program_id(2)

    @pl.when(i == 0)
    def _():
        if has_init:
            st_ref[...] = s0_ref[0]
        else:
            st_ref[...] = jnp.zeros(st_ref.shape, F32)

    chains = []
    for g in range(groups):
        ls = slice(g * LANES, (g + 1) * LANES)
        k_a = ka_ref[:, ls]
        chains.append(_scan_chain(fwd_refs, ls, k_a, False, st_ref, (0, g), yf_ref))
        chains.append(_scan_chain(bwd_refs, ls, k_a, True, st_ref, (1, g), yb_ref))
    _run_interleaved(chains)

    if want_state:
        @pl.when(i == pl.num_programs(2) - 1)
        def _():
            for z in range(2):
                for g in range(groups):
                    s = st_ref[z, g]
                    so_ref[0, z, 2 * g] = s[:HEAD, :HEAD]
                    so_ref[0, z, 2 * g + 1] = s[HEAD:, HEAD:]


def _rwkv_scan(r, k, v, kk, cl_f, cl_b, a_f, a_b, k_a, s0, want_state):
    b, t, width = r.shape
    pairs = width // LANES
    g = SCAN_PAIRS
    gw = g * LANES
    nt = t // TBLK
    fwd = pl.BlockSpec((1, TBLK, gw), lambda bi, p, i: (bi, i, p))
    bwd = pl.BlockSpec((1, TBLK, gw), lambda bi, p, i: (bi, nt - 1 - i, p))
    st_spec = pl.BlockSpec((1, 2, g, LANES, LANES), lambda bi, p, i: (bi, 0, p, 0, 0))
    specs = [fwd] * 6 + [bwd] * 6 + [pl.BlockSpec((1, gw), lambda bi, p, i: (0, p))]
    ins = [r, k, v, kk, cl_f, a_f, r, k, v, kk, cl_b, a_b, k_a.reshape(1, width)]
    if s0 is not None:
        specs.append(st_spec)
        ins.append(s0)
    y_shape = jax.ShapeDtypeStruct((b, t, width), F32)
    out_shape = [y_shape, y_shape]
    out_specs = [fwd, bwd]
    if want_state:
        out_shape.append(jax.ShapeDtypeStruct((b, 2, 2 * pairs, HEAD, HEAD), F32))
        out_specs.append(pl.BlockSpec((1, 2, 2 * g, HEAD, HEAD), lambda bi, p, i: (bi, 0, p, 0, 0)))
    kern = functools.partial(_scan_kernel, has_init=s0 is not None, want_state=want_state, groups=g)
    return pl.pallas_call(
        kern,
        out_shape=out_shape,
        grid=(b, pairs // g, nt),
        in_specs=specs,
        out_specs=out_specs,
        scratch_shapes=[pltpu.VMEM((2, g, LANES, LANES), F32)],
        compiler_params=_cparams(("arbitrary", "arbitrary", "arbitrary")),
        name="rwkv7_scan",
    )(*ins)


def _pair_states(s):
    b, z, h, n, _ = s.shape
    sp = s.reshape(b, z, h // 2, 2, n, n)
    zero = jnp.zeros_like(sp[:, :, :, 0])
    top = jnp.concatenate([sp[:, :, :, 0], zero], axis=-1)
    bot = jnp.concatenate([zero, sp[:, :, :, 1]], axis=-1)
    return jnp.concatenate([top, bot], axis=-2)


def _post_kernel(yf_ref, yb_ref, r_ref, k_ref, v_ref, gf_ref, gr_ref, g0_ref, g1_ref, fo_ref, x_ref, mod_ref,
                 vec_ref, seg_ref, wf_ref, wr_ref, wo_ref, o_ref, z_ref, *, cw):
    width = yf_ref.shape[2]
    inv_n = 1.0 / HEAD
    for j in range(width // cw):
        sl = slice(j * cw, (j + 1) * cw)
        seg = seg_ref[...]
        y = yf_ref[0, :, sl] + yb_ref[0, :, sl]
        mean = _dot_exact_rhs(y, seg) * inv_n
        d = y - mean
        var = _dot_exact_rhs(d * d, seg) * inv_n
        yn = d * lax.rsqrt(var + GN_EPS) * vec_ref[0:1, sl] + vec_ref[1:2, sl]
        rk = _dot_exact_rhs(r_ref[0, :, sl] * k_ref[0, :, sl] * vec_ref[2:3, sl], seg)
        z = (yn + rk * v_ref[0, :, sl]) * _silu(gr_ref[0, :, sl])
        z_ref[:, sl] = z.astype(BF16)
    out_r = jnp.dot(z_ref[...], wr_ref[...], preferred_element_type=F32)
    zf = (fo_ref[0] * _silu(gf_ref[0])).astype(BF16)
    out_f = jnp.dot(zf, wf_ref[...], preferred_element_type=F32)
    merged = jax.nn.sigmoid(g0_ref[0]) * out_f + jax.nn.sigmoid(g1_ref[0]) * out_r
    o = jnp.dot(merged.astype(BF16), wo_ref[...], preferred_element_type=F32)
    xn = x_ref[0] + mod_ref[0][2:3] * o
    o_ref[0] = xn * lax.rsqrt(jnp.mean(xn * xn, axis=-1, keepdims=True) + RMS_EPS) * vec_ref[3:4, :]


def _post(yf, yb, r, k, v, u, fo, x, mod, lnx_g, lnx_b, r_k, final_g, w_proj_f, w_proj_r, w_out, tm):
    b, t, d = x.shape
    width = yf.shape[2]
    fw = fo.shape[2]
    cw = MXU_DIM
    vec = jnp.stack([lnx_g, lnx_b, r_k.reshape(-1), final_g], axis=0)
    tok = lambda w, c: pl.BlockSpec((1, tm, w), lambda bi, i: (bi, i, c))
    const = lambda shape: pl.BlockSpec(shape, lambda bi, i: (0,) * len(shape), pipeline_mode=pl.Buffered(1))
    gr_off = 2 * fw + 3 * width
    mg_off = gr_off + width
    assert gr_off % width == 0 and mg_off % d == 0
    gr_blk, mg_blk = gr_off // width, mg_off // d
    specs = [tok(width, 0)] * 5 + [tok(fw, 1), tok(width, gr_blk), tok(d, mg_blk), tok(d, mg_blk + 1),
                                   tok(fw, 0), tok(d, 0), pl.BlockSpec((1, SUBLANES, d), lambda bi, i: (bi, 0, 0)),
                                   const((4, d)), const((cw, cw)), const((fw, d)), const((width, d)),
                                   const((d, d))]
    return pl.pallas_call(
        functools.partial(_post_kernel, cw=cw),
        out_shape=jax.ShapeDtypeStruct((b, t, d), F32),
        grid=(b, t // tm),
        in_specs=specs,
        out_specs=tok(d, 0),
        scratch_shapes=[pltpu.VMEM((tm, width), BF16)],
        compiler_params=_cparams(("arbitrary", "arbitrary")),
        name="post_mix",
    )(yf, yb, r, k, v, u, u, u, u, fo, x, mod, vec, _seg_matrix(cw),
      w_proj_f.astype(BF16), w_proj_r.astype(BF16), w_out.astype(BF16))


def _mixer_path(x, mod, s0, want_state, grid_shift, p, tiles, shared_mod):
    b, t, d = x.shape
    tokens = (lambda a: a.reshape(1, b * t, a.shape[2])) if shared_mod else (lambda a: a)
    u_tok = _inproj(tokens(x), mod, p["norm_g"], p["w_in"], tiles["inproj"], p["w_in_segments"])
    u = u_tok.reshape(b, t, u_tok.shape[2])
    fo = _fourier(u, t, p["fourier_width"], tiles["fourier_n1"], tiles["fourier"])
    r, k, v, kk, cl_f, cl_b, a_f, a_b = _rwkv_prep(
        u, p["mu_shift"], p["k_k"], p["w_up"], p["a_up"], p["w0"], p["a0"], grid_shift, tiles["prep"])
    res = _rwkv_scan(r, k, v, kk, cl_f, cl_b, a_f, a_b, p["k_a"], s0, want_state)
    y = _post(tokens(res[0]), tokens(res[1]), tokens(r), tokens(k), tokens(v), u_tok, tokens(fo), tokens(x), mod,
              p["lnx_g"], p["lnx_b"], p["r_k"], p["final_g"], p["w_proj_f"], p["w_proj_r"], p["w_out"],
              tiles["post"])
    return y.reshape(b, t, d), (res[2] if want_state else None)


def _tiles(t, grid_shift):
    token_tile = 2 * MXU_DIM
    if not grid_shift:
        return dict(inproj=token_tile, fourier=t, fourier_n1=1, prep=t, post=token_tile)
    return dict(inproj=token_tile, fourier=MXU_DIM, fourier_n1=4, prep=token_tile, post=token_tile)


def kernel(x_prompt, x_sample, state_rwkv, c, c_ctx, norm_g, w_ada, b_ada, w_in, mu_shift, w0, w_up, a0,
           a_up, k_k, k_a, r_k, lnx_g, lnx_b, w_proj_f, w_proj_r, w_out, final_g):
    depth = w_in.shape[0]
    assert depth == 1, "the final norm is fused into the single layer's post kernel"
    bp, tp, d = x_prompt.shape
    bs, ts, _ = x_sample.shape
    width = k_k.shape[1]
    fw = w_proj_f.shape[1]
    l = 0
    sh_end = 2 * fw + 3 * width
    rank2 = mu_shift.shape[1] - 3 * width
    n_in = w_in.shape[2]
    segments = ((0, 0, sh_end), (sh_end + rank2, sh_end, n_in - sh_end - rank2), (sh_end, n_in - rank2, rank2))
    p = dict(norm_g=norm_g[l], w_in=w_in[l].astype(BF16), w_in_segments=segments, fourier_width=fw,
             mu_shift=mu_shift[l], k_k=k_k[l], w_up=w_up[l], a_up=a_up[l], w0=w0[l], a0=a0[l], k_a=k_a[l],
             lnx_g=lnx_g[l], lnx_b=lnx_b[l], r_k=r_k[l], final_g=final_g, w_proj_f=w_proj_f[l],
             w_proj_r=w_proj_r[l], w_out=w_out[l])

    assert 1 + bs <= SUBLANES
    cvec = jnp.concatenate([c_ctx[None], c, jnp.zeros((SUBLANES - 1 - bs, d), F32)], axis=0)
    m3 = _modulation(cvec, w_ada[l], b_ada[l]).reshape(SUBLANES, 3, d)
    mod = jnp.concatenate([m3, jnp.zeros((SUBLANES, SUBLANES - 3, d), F32)], axis=1)

    yp, sp = _mixer_path(x_prompt, mod[0:1], None, True, False, p, _tiles(tp, False), shared_mod=True)
    s0 = _pair_states(state_rwkv[:, l])
    ys, _ = _mixer_path(x_sample, mod[1:1 + bs], s0, False, True, p, _tiles(ts, True), shared_mod=False)
    new_state = sp[:, None]
    return yp, ys, new_state
```
